```python
import math
import jax, jax.numpy as jnp
from jax import lax
import numpy as np

D_MODEL = 1024
BATCH = 8
SEQ = 4096
DEPTH = 1
DEC_BATCH = 128
DEC_SEQ = 8
PAST_LEN = 8192
PAGE_SIZE = 128

H_RET = 4
DK_RET = D_MODEL // 8
DV_RET = D_MODEL // 8
RET_CHUNK = 128
ROPE_BASE = 10000.0
RET_QK_W = H_RET * DK_RET
RET_V_W = H_RET * DV_RET
ATT_GROUPS = ((128, 1), (512, 4), (2048, 16))
N_GROUPS = 3
H_ATT = 8
DH_ATT = D_MODEL // 16
ATT_SPAN = 128
ATT_QKV_W = N_GROUPS * H_ATT * DH_ATT
ATT_OUT_W = H_ATT * DH_ATT
NEG = -1e30
N_BUCKETS = 32
MAX_DISTANCE = 2048
N_KEYS = 128
N_EXPERTS = N_KEYS * N_KEYS
H_PEER = 8
DK_PEER = 256
TOPK_PEER = 16
PEER_BLOCK = 256
ALPHA = (2 * DEPTH) ** 0.25
BETA = (8 * DEPTH) ** -0.25
LN_EPS = 1e-5
HN_EPS = 1e-6
IN_SPLITS = (RET_QK_W, RET_QK_W, RET_V_W, RET_V_W, ATT_QKV_W, ATT_QKV_W, ATT_QKV_W, D_MODEL, D_MODEL)
IN_COLS = RET_QK_W * 2 + RET_V_W * 2 + ATT_QKV_W * 3 + D_MODEL * 2

kernel_name = "hybrid_retention_dilated_attn_peer_decode_step"


def _layer_norm(x, g, b):
    xf = x.astype(jnp.float32)
    mu = xf.mean(-1, keepdims=True)
    var = jnp.mean(jnp.square(xf - mu), -1, keepdims=True)
    return ((xf - mu) * lax.rsqrt(var + LN_EPS) * g + b).astype(x.dtype)


def _head_norm(o):
    mu = o.mean(-1, keepdims=True)
    var = jnp.mean(jnp.square(o - mu), -1, keepdims=True)
    return (o - mu) * lax.rsqrt(var + HN_EPS)


def _ada(c, w_ada, b_ada):
    mod = jax.nn.silu(c) @ w_ada + b_ada
    return jnp.split(mod[:, None, :], 6, axis=-1)


def _project(h, w_in):
    p = h @ w_in
    offs = np.cumsum(IN_SPLITS)[:-1].tolist()
    return jnp.split(p, offs, axis=-1)


def _rotary(x, pos):
    half = x.shape[-1] // 2
    inv = ROPE_BASE ** (-jnp.arange(half, dtype=jnp.float32) / half)
    ang = pos.astype(jnp.float32)[:, None] * inv[None, :]
    cos = jnp.cos(ang)[None, :, None, :]
    sin = jnp.sin(ang)[None, :, None, :]
    xf = x.astype(jnp.float32)
    x1, x2 = xf[..., :half], xf[..., half:]
    return jnp.concatenate([x1 * cos - x2 * sin, x1 * sin + x2 * cos], axis=-1)


def _retention_chunkwise(q, k, v, s0):
    n, L, H, dk = q.shape
    dv = v.shape[-1]
    C = math.gcd(L, RET_CHUNK)
    nc = L // C
    lg = jnp.log1p(-jnp.exp2(-5.0 - jnp.arange(H, dtype=jnp.float32)))
    idx = jnp.arange(C, dtype=jnp.float32)
    rel = idx[:, None] - idx[None, :]
    intra = jnp.where(rel >= 0, jnp.exp(lg[:, None, None] * jnp.maximum(rel, 0.0)), 0.0)
    q_dec = jnp.exp(lg[:, None] * (idx[None, :] + 1.0)).T[None, :, :, None]
    k_dec = jnp.exp(lg[:, None] * (C - 1.0 - idx[None, :])).T[None, :, :, None]
    chunk_dec = jnp.exp(lg * C)[None, :, None, None]

    def to_chunks(a):
        a = a.astype(jnp.float32)
        return a.reshape(n, nc, C, H, a.shape[-1]).transpose(1, 0, 2, 3, 4)

    def step(s, inp):
        qc, kc, vc = inp
        scores = jnp.einsum('bihd,bjhd->bhij', qc, kc) * intra
        o = jnp.einsum('bhij,bjhe->bihe', scores, vc)
        o = o + jnp.einsum('bihd,bhde->bihe', qc, s) * q_dec
        s = s * chunk_dec + jnp.einsum('bjhd,bjhe->bhde', kc * k_dec, vc)
        return s, o

    s, o = lax.scan(step, s0.astype(jnp.float32), (to_chunks(q), to_chunks(k), to_chunks(v)))
    return o.transpose(1, 0, 2, 3, 4).reshape(n, L, H, dv), s


def _t5_bucket(dist):
    max_exact = N_BUCKETS // 2
    d = dist.astype(jnp.float32)
    large = max_exact + (jnp.log(jnp.maximum(d, 1.0) / max_exact)
                         / math.log(MAX_DISTANCE / max_exact) * (N_BUCKETS - max_exact))
    large = jnp.minimum(large.astype(jnp.int32), N_BUCKETS - 1)
    return jnp.where(dist < max_exact, dist, large)


def _dilated_attn_prompt(q, k, v, dil, bias_tab):
    B, L, H, Dh = q.shape
    n = L // dil
    blk = ATT_SPAN
    nb = -(-n // blk)
    npad = nb * blk
    bd = B * dil

    def by_residue(a):
        return a.reshape(B, n, dil, H, Dh).transpose(0, 2, 1, 3, 4).reshape(bd, n, H, Dh)

    qb = jnp.pad(by_residue(q), ((0, 0), (0, npad - n), (0, 0), (0, 0))).reshape(bd, nb, blk, H, Dh)

    def band(a):
        ap = jnp.pad(by_residue(a), ((0, 0), (blk, npad - n), (0, 0), (0, 0))).reshape(bd, nb + 1, blk, H, Dh)
        return jnp.concatenate([ap[:, :-1], ap[:, 1:]], axis=2)

    kb, vb = band(k), band(v)
    a_ = jnp.arange(blk)[:, None]
    b_ = jnp.arange(2 * blk)[None, :]
    rel = a_ + blk - b_
    kpos = jnp.arange(nb)[:, None, None] * blk + b_[None] - blk
    valid = (rel >= 0) & (rel <= ATT_SPAN) & (kpos >= 0)
    bias = bias_tab[_t5_bucket(jnp.clip(rel, 0, None) * dil)].transpose(2, 0, 1)
    logits = jnp.einsum('znqhd,znkhd->znhqk', qb, kb).astype(jnp.float32) * (Dh ** -0.5) + bias[None, None]
    logits = jnp.where(valid[None, :, None], logits, NEG)
    m = logits.max(-1, keepdims=True)
    p = jnp.exp(logits - m)
    l = p.sum(-1, keepdims=True)
    o = jnp.einsum('znhqk,znkhd->znqhd', (p / l).astype(v.dtype), vb)
    lse = (m + jnp.log(l))[..., 0].transpose(0, 1, 3, 2)
    o = o.reshape(bd, npad, H, Dh)[:, :n].reshape(B, dil, n, H, Dh).transpose(0, 2, 1, 3, 4).reshape(B, L, H, Dh)
    lse = lse.reshape(bd, npad, H)[:, :n].reshape(B, dil, n, H).transpose(0, 2, 1, 3).reshape(B, L, H)
    return o, lse


def _dilated_attn_sample(q, k_all, v_all, n_buf, dil, bias_tab):
    T = q.shape[1]
    j = jnp.arange(ATT_SPAN + 1)
    idx = n_buf + jnp.arange(T)[:, None] - j[None, :] * dil
    valid = idx >= 0
    idx_c = jnp.maximum(idx, 0)
    kg = k_all[:, idx_c]
    vg = v_all[:, idx_c]
    bias = bias_tab[_t5_bucket(j * dil)].T
    logits = jnp.einsum('bthd,btjhd->bthj', q, kg).astype(jnp.float32) * (q.shape[-1] ** -0.5) + bias[None, None]
    logits = jnp.where(valid[None, :, None, :], logits, NEG)
    m = logits.max(-1, keepdims=True)
    p = jnp.exp(logits - m)
    l = p.sum(-1, keepdims=True)
    o = jnp.einsum('bthj,btjhd->bthd', (p / l).astype(vg.dtype), vg)
    return o, (m + jnp.log(l))[..., 0]


def _peer(h, w_q, sub_keys, u_tab, v_tab):
    shp = h.shape
    t = h.reshape(-1, D_MODEL)
    n = t.shape[0]
    nb = -(-n // PEER_BLOCK)
    tp = jnp.pad(t, ((0, nb * PEER_BLOCK - n), (0, 0))).reshape(nb, PEER_BLOCK, D_MODEL)

    def block(xb):
        q = (xb @ w_q).reshape(PEER_BLOCK, H_PEER, 2, DK_PEER // 2)
        s = jnp.einsum('thsc,hskc->thsk', q, sub_keys).astype(jnp.float32)
        sv, si = lax.top_k(s, TOPK_PEER)
        cand = (sv[:, :, 0, :, None] + sv[:, :, 1, None, :]).reshape(PEER_BLOCK, H_PEER, TOPK_PEER * TOPK_PEER)
        cv, ci = lax.top_k(cand, TOPK_PEER)
        i0 = jnp.take_along_axis(si[:, :, 0], ci // TOPK_PEER, axis=-1)
        i1 = jnp.take_along_axis(si[:, :, 1], ci % TOPK_PEER, axis=-1)
        e = i0 * N_KEYS + i1
        g = jax.nn.softmax(cv, axis=-1)
        a = jax.nn.gelu(jnp.einsum('td,thkd->thk', xb, u_tab[e]), approximate=False)
        return jnp.einsum('thk,thkd->td', (g * a).astype(xb.dtype), v_tab[e])

    y = lax.map(block, tp).reshape(-1, D_MODEL)[:n]
    return y.reshape(shp)


def _layer(x, c, pos0, ret_s0, att_bufs, w_ada, b_ada, w_in, w_ret_o, w_att_o, w_out, b_out,
           ln1_g, ln1_b, peer_wq, peer_keys, peer_u, peer_v, ln2_g, ln2_b, rel_bias):
    nb, L, _ = x.shape
    sh1, sc1, gt1, sh2, sc2, gt2 = _ada(c, w_ada, b_ada)
    h = x * (1 + sc1) + sh1
    rq, rk, rv, rg, aq, ak, av, g_ret, g_att = _project(h, w_in)
    pos = pos0 + jnp.arange(L, dtype=jnp.int32)
    rq = _rotary(rq.reshape(nb, L, H_RET, DK_RET), pos)
    rk = _rotary(rk.reshape(nb, L, H_RET, DK_RET), pos) * (DK_RET ** -0.5)
    rv = rv.reshape(nb, L, H_RET, DV_RET)
    if ret_s0 is None:
        ret_s0 = jnp.zeros((nb, H_RET, DK_RET, DV_RET), jnp.float32)
    ret_o, ret_s = _retention_chunkwise(rq, rk, rv, ret_s0)
    ret_y = (jax.nn.silu(rg) * _head_norm(ret_o).reshape(nb, L, RET_V_W).astype(x.dtype)) @ w_ret_o
    aq = aq.reshape(nb, L, N_GROUPS, H_ATT, DH_ATT)
    ak = ak.reshape(nb, L, N_GROUPS, H_ATT, DH_ATT)
    av = av.reshape(nb, L, N_GROUPS, H_ATT, DH_ATT)
    outs, lses, new_bufs = [], [], []
    for g, (win, dil) in enumerate(ATT_GROUPS):
        q_g, k_g, v_g = aq[:, :, g], ak[:, :, g], av[:, :, g]
        tab = rel_bias[:, g * H_ATT:(g + 1) * H_ATT]
        if att_bufs is None:
            o, lse = _dilated_attn_prompt(q_g, k_g, v_g, dil, tab)
            nw = min(win, L)
            new_bufs.append(jnp.stack([k_g[:, L - nw:], v_g[:, L - nw:]], axis=2))
        else:
            buf = att_bufs[g]
            n_buf = buf.shape[1]
            k_all = jnp.concatenate([buf[:, :, 0].astype(k_g.dtype), k_g], axis=1)
            v_all = jnp.concatenate([buf[:, :, 1].astype(v_g.dtype), v_g], axis=1)
            o, lse = _dilated_attn_sample(q_g, k_all, v_all, n_buf, dil, tab)
            new_bufs.append(jnp.stack([k_all[:, L:], v_all[:, L:]], axis=2))
        outs.append(o)
        lses.append(lse)
    w_grp = jax.nn.softmax(jnp.stack(lses, 0), axis=0)
    att_o = jnp.einsum('gblh,gblhd->blhd', w_grp.astype(x.dtype), jnp.stack(outs, 0))
    att_y = att_o.reshape(nb, L, ATT_OUT_W) @ w_att_o
    mix = (jax.nn.sigmoid(g_ret) * ret_y + jax.nn.sigmoid(g_att) * att_y) @ w_out + b_out
    x = _layer_norm(ALPHA * x + gt1 * mix, ln1_g, ln1_b)
    h2 = x * (1 + sc2) + sh2
    x = _layer_norm(ALPHA * x + gt2 * _peer(h2, peer_wq, peer_keys, peer_u, peer_v), ln2_g, ln2_b)
    return x, ret_s, new_bufs


def setup_inputs(seed: int = 0) -> dict:
    key = jax.random.key(seed)
    ks = jax.random.split(key, 24)
    f32 = jnp.float32

    def nrm(k, shape, s):
        return jax.random.normal(k, shape, f32) * s

    nbuf = [min(w, PAST_LEN) for (w, _) in ATT_GROUPS]
    return {
        "x_prompt": nrm(ks[0], (BATCH, SEQ, D_MODEL), 1.0),
        "x_sample": nrm(ks[1], (DEC_BATCH, DEC_SEQ, D_MODEL), 1.0),
        "state_ret": nrm(ks[2], (DEPTH, DEC_BATCH, H_RET, DK_RET, DV_RET), 0.5),
        "cache_att_w128": nrm(ks[3], (DEPTH, DEC_BATCH, nbuf[0], 2, H_ATT, DH_ATT), 1.0),
        "cache_att_w512": nrm(ks[4], (DEPTH, DEC_BATCH, nbuf[1], 2, H_ATT, DH_ATT), 1.0),
        "cache_att_w2048": nrm(ks[5], (DEPTH, DEC_BATCH, nbuf[2], 2, H_ATT, DH_ATT), 1.0),
        "c_prompt": nrm(ks[6], (BATCH, D_MODEL), 1.0),
        "c_sample": nrm(ks[7], (DEC_BATCH, D_MODEL), 1.0),
        "w_ada": nrm(ks[8], (DEPTH, D_MODEL, 6 * D_MODEL), 0.5 * D_MODEL ** -0.5),
        "b_ada": nrm(ks[9], (DEPTH, 6 * D_MODEL), 0.02),
        "w_in": nrm(ks[10], (DEPTH, D_MODEL, IN_COLS), D_MODEL ** -0.5),
        "w_ret_o": nrm(ks[11], (DEPTH, RET_V_W, D_MODEL), BETA * RET_V_W ** -0.5),
        "w_att_o": nrm(ks[12], (DEPTH, ATT_OUT_W, D_MODEL), BETA * ATT_OUT_W ** -0.5),
        "w_out": nrm(ks[13], (DEPTH, D_MODEL, D_MODEL), BETA * D_MODEL ** -0.5),
        "b_out": nrm(ks[14], (DEPTH, D_MODEL), 0.02),
        "ln1_g": 1.0 + nrm(ks[15], (DEPTH, D_MODEL), 0.02),
        "ln1_b": nrm(ks[16], (DEPTH, D_MODEL), 0.02),
        "peer_wq": nrm(ks[17], (DEPTH, D_MODEL, H_PEER * DK_PEER), D_MODEL ** -0.5),
        "peer_keys": nrm(ks[18], (DEPTH, H_PEER, 2, N_KEYS, DK_PEER // 2), (DK_PEER // 2) ** -0.5),
        "peer_u": nrm(ks[19], (DEPTH, N_EXPERTS, D_MODEL), D_MODEL ** -0.5),
        "peer_v": nrm(ks[20], (DEPTH, N_EXPERTS, D_MODEL), BETA),
        "ln2_g": 1.0 + nrm(ks[21], (DEPTH, D_MODEL), 0.02),
        "ln2_b": nrm(ks[22], (DEPTH, D_MODEL), 0.02),
        "rel_bias": nrm(ks[23], (N_BUCKETS, N_GROUPS * H_ATT), 0.5),
    }


def reference(x_prompt, x_sample, state_ret, cache_att_w128, cache_att_w512, cache_att_w2048,
              c_prompt, c_sample, w_ada, b_ada, w_in, w_ret_o, w_att_o, w_out, b_out,
              ln1_g, ln1_b, peer_wq, peer_keys, peer_u, peer_v, ln2_g, ln2_b, rel_bias):
    yp, ys = x_prompt, x_sample
    rp, rs = [], []
    bp = [[] for _ in range(N_GROUPS)]
    bs = [[] for _ in range(N_GROUPS)]
    for l in range(DEPTH):
        w = (w_ada[l], b_ada[l], w_in[l], w_ret_o[l], w_att_o[l], w_out[l], b_out[l],
             ln1_g[l], ln1_b[l], peer_wq[l], peer_keys[l], peer_u[l], peer_v[l],
             ln2_g[l], ln2_b[l], rel_bias)
        yp, r_p, bufs_p = _layer(yp, c_prompt, 0, None, None, *w)
        ys, r_s, bufs_s = _layer(ys, c_sample, PAST_LEN, state_ret[l],
                                 (cache_att_w128[l], cache_att_w512[l], cache_att_w2048[l]), *w)
        rp.append(r_p)
        rs.append(r_s)
        for g in range(N_GROUPS):
            bp[g].append(bufs_p[g])
            bs[g].append(bufs_s[g])
    new_state_ret_prompt = jnp.stack(rp, 0)
    new_cache_w128_prompt = jnp.stack(bp[0], 0)
    new_cache_w512_prompt = jnp.stack(bp[1], 0)
    new_cache_w2048_prompt = jnp.stack(bp[2], 0)
    new_state_ret_sample = jnp.stack(rs, 0)
    new_cache_w128_sample = jnp.stack(bs[0], 0)
    new_cache_w512_sample = jnp.stack(bs[1], 0)
    new_cache_w2048_sample = jnp.stack(bs[2], 0)
    return (yp, ys, new_state_ret_prompt, new_cache_w128_prompt, new_cache_w512_prompt, new_cache_w2048_prompt,
            new_state_ret_sample, new_cache_w128_sample, new_cache_w512_sample, new_cache_w2048_sample)
```

```python
import functools
import math

import numpy as np
import jax
import jax.numpy as jnp
from jax import lax
from jax.experimental import pallas as pl
from jax.experimental.pallas import tpu as pltpu

F32 = jnp.float32
BF16 = jnp.bfloat16

D_MODEL = 1024
PAST_LEN = 8192
H_RET = 4
DK_RET = 128
RET_CHUNK = 128
ROPE_BASE = 10000.0
ATT_GROUPS = ((128, 1), (512, 4), (2048, 16))
N_GROUPS = 3
H_ATT = 8
DH_ATT = 64
ATT_SPAN = 128
ATT_W = H_ATT * DH_ATT
NEG = -1e30
N_BUCKETS = 32
MAX_DISTANCE = 2048
N_KEYS = 128
H_PEER = 8
TOPK = 16
N_HS = 2 * H_PEER
N_SEL = H_PEER * TOPK
ALPHA = 2.0 ** 0.25
LN_EPS = 1e-5
HN_EPS = 1e-6
IN_COLS = 8704
COL_BLK = 512
CB_RQ, CB_RK, CB_RV, CB_RG = 0, 1, 2, 3
CB_AQ, CB_AK, CB_AV = 4, 7, 10
CB_GRET, CB_GATT = 13, 15
N_CB = IN_COLS // COL_BLK
LOG_GAMMA = [math.log1p(-(2.0 ** (-5.0 - h))) for h in range(H_RET)]
PEER_TOK = 8
VMEM_LIMIT = 56 * 1024 * 1024


def _cparams(n_axes, vmem=None):
    return pltpu.CompilerParams(dimension_semantics=("arbitrary",) * n_axes,
                                vmem_limit_bytes=vmem)


def _dot(a, b):
    return jnp.dot(a, b, preferred_element_type=F32)


def _dot_nt(a, b):
    return lax.dot_general(a, b, (((1,), (1,)), ((), ())), preferred_element_type=F32)


def _dot_tn(a, b):
    return lax.dot_general(a, b, (((0,), (0,)), ((), ())), preferred_element_type=F32)


def _dot_exact(x, sel):
    hi = x.astype(BF16)
    r1 = x - hi.astype(F32)
    mid = r1.astype(BF16)
    lo = (r1 - mid.astype(F32)).astype(BF16)
    return _dot(hi, sel) + _dot(mid, sel) + _dot(lo, sel)


def _layer_norm(y, g, b):
    mu = jnp.mean(y, axis=-1, keepdims=True)
    d = y - mu
    var = jnp.mean(d * d, axis=-1, keepdims=True)
    return d * lax.rsqrt(var + LN_EPS) * g + b


def _ada_kernel(c_ref, w_ref, b_ref, o_ref):
    c = c_ref[...]
    s = c * jax.nn.sigmoid(c)
    o_ref[...] = _dot(s.astype(BF16), w_ref[...].astype(BF16)) + b_ref[...]


def _ada(c, w, b):
    n = c.shape[0]
    return pl.pallas_call(
        _ada_kernel,
        grid=(6,),
        in_specs=[pl.BlockSpec((n, D_MODEL), lambda j: (0, 0)),
                  pl.BlockSpec((D_MODEL, D_MODEL), lambda j: (0, j)),
                  pl.BlockSpec((1, D_MODEL), lambda j: (0, j))],
        out_specs=pl.BlockSpec((n, D_MODEL), lambda j: (0, j)),
        out_shape=jax.ShapeDtypeStruct((n, 6 * D_MODEL), F32),
        compiler_params=_cparams(1, VMEM_LIMIT),
        name="ada",
    )(c, w, b.reshape(1, -1))


def _inproj_kernel(x_ref, sh_ref, sc_ref, w_ref, o_ref, h_scr):
    @pl.when(pl.program_id(2) == 0)
    def _():
        h = x_ref[...] * (1.0 + sc_ref[...]) + sh_ref[...]
        h_scr[...] = h.reshape(h_scr.shape).astype(BF16)

    o_ref[...] = _dot(h_scr[...], w_ref[...]).reshape(o_ref.shape)


def _inproj(x, mod, w_b, bn, rb):
    nb, r, _ = x.shape
    return pl.pallas_call(
        _inproj_kernel,
        grid=(nb // bn, r // rb, N_CB),
        in_specs=[pl.BlockSpec((bn, rb, D_MODEL), lambda a, b, j: (a, b, 0)),
                  pl.BlockSpec((bn, 1, D_MODEL), lambda a, b, j: (a, 0, 0)),
                  pl.BlockSpec((bn, 1, D_MODEL), lambda a, b, j: (a, 0, 1)),
                  pl.BlockSpec((D_MODEL, COL_BLK), lambda a, b, j: (0, j))],
        out_specs=pl.BlockSpec((bn, rb, COL_BLK), lambda a, b, j: (a, b, j)),
        out_shape=jax.ShapeDtypeStruct((nb, r, IN_COLS), F32),
        scratch_shapes=[pltpu.VMEM((bn * rb, D_MODEL), BF16)],
        compiler_params=_cparams(3, VMEM_LIMIT),
        name="inproj",
    )(x, mod, mod, w_b)


def _ret_kernel(q_ref, k_ref, v_ref, g_ref, cos_ref, sin_ref, s0_ref, z_ref, s_ref, *, bn, chunk):
    @pl.when(pl.program_id(1) == 0)
    def _():
        s_ref[...] = s0_ref[...]

    cos = cos_ref[...]
    sin = sin_ref[...]
    ri = lax.broadcasted_iota(jnp.int32, (chunk, chunk), 0)
    rj = lax.broadcasted_iota(jnp.int32, (chunk, chunk), 1)
    rel = (ri - rj).astype(F32)
    ci = lax.broadcasted_iota(jnp.int32, (chunk, 1), 0).astype(F32)
    for b in range(bn):
        for h in range(H_RET):
            sl = slice(h * DK_RET, (h + 1) * DK_RET)
            lg = LOG_GAMMA[h]
            q = q_ref[b, :, sl]
            k = k_ref[b, :, sl]
            v = v_ref[b, :, sl]
            qr = q * cos + pltpu.roll(q, DK_RET // 2, 1) * sin
            kr = (k * cos + pltpu.roll(k, DK_RET // 2, 1) * sin) * (DK_RET ** -0.5)
            intra = jnp.where(rel >= 0, jnp.exp(lg * jnp.maximum(rel, 0.0)), 0.0)
            scores = _dot_nt(qr, kr) * intra
            s = s_ref[b, h]
            o = _dot(scores, v) + _dot(qr, s) * jnp.exp(lg * (ci + 1.0))
            k_dec = jnp.exp(lg * (chunk - 1.0 - ci))
            s_ref[b, h] = s * math.exp(lg * chunk) + _dot_tn(kr * k_dec, v)
            mu = jnp.mean(o, axis=-1, keepdims=True)
            d = o - mu
            var = jnp.mean(d * d, axis=-1, keepdims=True)
            g = g_ref[b, :, sl]
            z_ref[b, :, sl] = g * jax.nn.sigmoid(g) * (d * lax.rsqrt(var + HN_EPS))


def _retention(p, cos, sin, s0, bn, chunk):
    nb, r, _ = p.shape

    def col(cb):
        return pl.BlockSpec((bn, chunk, COL_BLK), lambda a, c, cb=cb: (a, c, cb))

    st_spec = pl.BlockSpec((bn, H_RET, DK_RET, DK_RET), lambda a, c: (a, 0, 0, 0))
    return pl.pallas_call(
        functools.partial(_ret_kernel, bn=bn, chunk=chunk),
        grid=(nb // bn, r // chunk),
        in_specs=[col(CB_RQ), col(CB_RK), col(CB_RV), col(CB_RG),
                  pl.BlockSpec((chunk, DK_RET), lambda a, c: (c, 0)),
                  pl.BlockSpec((chunk, DK_RET), lambda a, c: (c, 0)),
                  st_spec],
        out_specs=[pl.BlockSpec((bn, chunk, COL_BLK), lambda a, c: (a, c, 0)), st_spec],
        out_shape=[jax.ShapeDtypeStruct((nb, r, COL_BLK), F32),
                   jax.ShapeDtypeStruct((nb, H_RET, DK_RET, DK_RET), F32)],
        compiler_params=_cparams(2, VMEM_LIMIT),
        name="retention",
    )(p, p, p, p, cos, sin, s0)


def _rotary_tables(pos0, length):
    half = DK_RET // 2
    inv = ROPE_BASE ** (-jnp.arange(half, dtype=F32) / half)
    ang = (pos0 + jnp.arange(length, dtype=jnp.int32)).astype(F32)[:, None] * inv[None, :]
    cos, sin = jnp.cos(ang), jnp.sin(ang)
    return jnp.concatenate([cos, cos], axis=-1), jnp.concatenate([-sin, sin], axis=-1)


def _t5_bucket_np(dist):
    max_exact = N_BUCKETS // 2
    d = dist.astype(np.float32)
    large = np.float32(max_exact) + (np.log(np.maximum(d, np.float32(1.0)) / np.float32(max_exact))
                                     / np.float32(math.log(MAX_DISTANCE / max_exact))
                                     * np.float32(N_BUCKETS - max_exact))
    large = np.minimum(large.astype(np.int32), N_BUCKETS - 1)
    return np.where(dist < max_exact, dist, large).astype(np.int32)


BIAS_CHUNK = 1024


def _bias_kernel(tab_ref, idx_ref, o_ref):
    idx = idx_ref[...]
    acc = jnp.zeros(o_ref.shape, F32)
    for b in range(N_BUCKETS):
        acc = jnp.where(idx == b, tab_ref[:, b:b + 1], acc)
    o_ref[...] = acc


def _bias_lookup(tab_t, idx):
    n = idx.shape[1]
    return pl.pallas_call(
        _bias_kernel,
        grid=(n // BIAS_CHUNK,),
        in_specs=[pl.BlockSpec((H_ATT, N_BUCKETS), lambda j: (0, 0)),
                  pl.BlockSpec((1, BIAS_CHUNK), lambda j: (0, j))],
        out_specs=pl.BlockSpec((H_ATT, BIAS_CHUNK), lambda j: (0, j)),
        out_shape=jax.ShapeDtypeStruct((H_ATT, n), F32),
        compiler_params=_cparams(1),
        name="bias_lookup",
    )(tab_t, idx)


def _prompt_bucket_idx(dil):
    a = np.arange(ATT_SPAN)[:, None]
    b = np.arange(2 * ATT_SPAN)[None, :]
    rel = a + ATT_SPAN - b
    return _t5_bucket_np(np.clip(rel, 0, None) * dil).reshape(1, -1)


def _sample_bucket_idx(n_buf, n_pad, t_len):
    t = np.arange(t_len)[:, None]
    i = np.arange(n_pad)[None, :]
    return _t5_bucket_np(np.clip(n_buf + t - i, 0, None)).reshape(1, -1)


def _attn_p_kernel(q_ref, kp_ref, kc_ref, vp_ref, vc_ref, bias_ref, o_ref, l_ref):
    blk = ATT_SPAN
    first = pl.program_id(2) == 0
    a = lax.broadcasted_iota(jnp.int32, (blk, 2 * blk), 0)
    b = lax.broadcasted_iota(jnp.int32, (blk, 2 * blk), 1)
    rel = a + blk - b
    k_lo = jnp.where(first, blk, 0)
    valid = (rel >= 0) & (rel <= ATT_SPAN) & (b >= k_lo)
    for h in range(H_ATT):
        sl = slice(h * DH_ATT, (h + 1) * DH_ATT)
        q = q_ref[0, :, sl].astype(BF16)
        k = jnp.concatenate([kp_ref[0, :, sl], kc_ref[0, :, sl]], axis=0).astype(BF16)
        v = jnp.concatenate([vp_ref[0, :, sl], vc_ref[0, :, sl]], axis=0).astype(BF16)
        logits = _dot_nt(q, k) * (DH_ATT ** -0.5) + bias_ref[h]
        logits = jnp.where(valid, logits, NEG)
        m = jnp.max(logits, axis=-1, keepdims=True)
        p = jnp.exp(logits - m)
        l = jnp.sum(p, axis=-1, keepdims=True)
        o_ref[0, :, sl] = _dot(p.astype(BF16), v) / l
        l_ref[0, :, sl] = jnp.broadcast_to(m + jnp.log(l), (blk, DH_ATT))


def _attn_prompt(p, bias, g, dil):
    nb, seq, _ = p.shape
    n = seq // dil
    pv = p.reshape(nb, n, dil * IN_COLS)

    def cur(cb):
        return pl.BlockSpec((1, ATT_SPAN, COL_BLK), lambda bb, r, i, cb=cb: (bb, i, r * N_CB + cb))

    def prev(cb):
        return pl.BlockSpec((1, ATT_SPAN, COL_BLK),
                            lambda bb, r, i, cb=cb: (bb, jnp.maximum(i - 1, 0), r * N_CB + cb))

    out_spec = pl.BlockSpec((1, ATT_SPAN, ATT_W), lambda bb, r, i: (bb, i, r))
    o, l = pl.pallas_call(
        _attn_p_kernel,
        grid=(nb, dil, n // ATT_SPAN),
        in_specs=[cur(CB_AQ + g), prev(CB_AK + g), cur(CB_AK + g), prev(CB_AV + g), cur(CB_AV + g),
                  pl.BlockSpec((H_ATT, ATT_SPAN, 2 * ATT_SPAN), lambda bb, r, i: (0, 0, 0))],
        out_specs=[out_spec, out_spec],
        out_shape=[jax.ShapeDtypeStruct((nb, n, dil * ATT_W), F32)] * 2,
        compiler_params=_cparams(3, VMEM_LIMIT),
        name="attn_prompt",
    )(pv, pv, pv, pv, pv, bias)
    return o.reshape(nb, seq, ATT_W), l.reshape(nb, seq, ATT_W)


def _attn_s_kernel(q_ref, k_ref, v_ref, c_ref, bias_ref, o_ref, l_ref, nc_ref, kall, vall, *, n_buf, dil):
    t_len = q_ref.shape[1]
    n_pad = kall.shape[0]

    @pl.when(pl.program_id(0) == 0)
    def _():
        kall[n_buf + t_len:, :] = jnp.zeros((n_pad - n_buf - t_len, ATT_W), F32)
        vall[n_buf + t_len:, :] = jnp.zeros((n_pad - n_buf - t_len, ATT_W), F32)

    kall[0:n_buf, :] = c_ref[0, :, 0:ATT_W]
    vall[0:n_buf, :] = c_ref[0, :, ATT_W:2 * ATT_W]
    kall[n_buf:n_buf + t_len, :] = k_ref[0]
    vall[n_buf:n_buf + t_len, :] = v_ref[0]
    nc_ref[0, :, 0:ATT_W] = kall[t_len:n_buf + t_len, :]
    nc_ref[0, :, ATT_W:2 * ATT_W] = vall[t_len:n_buf + t_len, :]

    t = lax.broadcasted_iota(jnp.int32, (t_len, n_pad), 0)
    i = lax.broadcasted_iota(jnp.int32, (t_len, n_pad), 1)
    d = n_buf + t - i
    valid = (d >= 0) & ((d & (dil - 1)) == 0) & (d <= ATT_SPAN * dil)
    for h in range(H_ATT):
        sl = slice(h * DH_ATT, (h + 1) * DH_ATT)
        q = q_ref[0, :, sl].astype(BF16)
        k = kall[:, sl].astype(BF16)
        v = vall[:, sl].astype(BF16)
        logits = _dot_nt(q, k) * (DH_ATT ** -0.5) + bias_ref[h]
        logits = jnp.where(valid, logits, NEG)
        m = jnp.max(logits, axis=-1, keepdims=True)
        p = jnp.exp(logits - m)
        l = jnp.sum(p, axis=-1, keepdims=True)
        o_ref[0, :, sl] = _dot(p.astype(BF16), v) / l
        l_ref[0, :, sl] = jnp.broadcast_to(m + jnp.log(l), (t_len, DH_ATT))


def _attn_sample(p, cache, bias, g, dil):
    nb, t_len, _ = p.shape
    n_buf = cache.shape[1]
    n_pad = bias.shape[2]
    cv = cache.reshape(nb, n_buf, 2 * ATT_W)

    def col(cb):
        return pl.BlockSpec((1, t_len, COL_BLK), lambda n, cb=cb: (n, 0, cb))

    small = pl.BlockSpec((1, t_len, ATT_W), lambda n: (n, 0, 0))
    big = pl.BlockSpec((1, n_buf, 2 * ATT_W), lambda n: (n, 0, 0))
    o, l, nc = pl.pallas_call(
        functools.partial(_attn_s_kernel, n_buf=n_buf, dil=dil),
        grid=(nb,),
        in_specs=[col(CB_AQ + g), col(CB_AK + g), col(CB_AV + g), big,
                  pl.BlockSpec((H_ATT, t_len, n_pad), lambda n: (0, 0, 0))],
        out_specs=[small, small, big],
        out_shape=[jax.ShapeDtypeStruct((nb, t_len, ATT_W), F32)] * 2
                  + [jax.ShapeDtypeStruct((nb, n_buf, 2 * ATT_W), F32)],
        scratch_shapes=[pltpu.VMEM((n_pad, ATT_W), F32), pltpu.VMEM((n_pad, ATT_W), F32)],
        compiler_params=_cparams(1, VMEM_LIMIT),
        name="attn_sample",
    )(p, p, p, cv, bias)
    return o, l, nc


def _merge_kernel(o0, o1, o2, l0, l1, l2, z_ref, gra, grb, gaa, gab, x_ref, gt1_ref, sh2_ref, sc2_ref,
                  wro, wao, wout, bout, lng, lnb, wq, keys, x1_ref, st_ref):
    tm = st_ref.shape[1]

    def flat(ref):
        v = ref[...]
        return v.reshape(tm, v.shape[-1])

    la, lb, lc = flat(l0), flat(l1), flat(l2)
    m = jnp.maximum(jnp.maximum(la, lb), lc)
    ea, eb, ec = jnp.exp(la - m), jnp.exp(lb - m), jnp.exp(lc - m)
    att = (ea * flat(o0) + eb * flat(o1) + ec * flat(o2)) / (ea + eb + ec)
    att_y = _dot(att.astype(BF16), wao[...])
    ret_y = _dot(flat(z_ref).astype(BF16), wro[...])
    g_ret = jnp.concatenate([flat(gra), flat(grb)], axis=-1)
    g_att = jnp.concatenate([flat(gaa), flat(gab)], axis=-1)
    zz = jax.nn.sigmoid(g_ret) * ret_y + jax.nn.sigmoid(g_att) * att_y
    mix = _dot(zz.astype(BF16), wout[...]) + bout[...]
    x = x_ref[...]
    y = ALPHA * x + gt1_ref[...] * mix.reshape(x.shape)
    x1 = _layer_norm(y, lng[...], lnb[...])
    x1_ref[...] = x1
    h2 = (x1 * (1.0 + sc2_ref[...]) + sh2_ref[...]).reshape(tm, D_MODEL)
    q = _dot(h2.astype(BF16), wq[...]).astype(BF16)
    for hs in range(N_HS):
        st_ref[hs * N_KEYS:(hs + 1) * N_KEYS, :] = _dot_nt(keys[hs], q[:, hs * N_KEYS:(hs + 1) * N_KEYS])


def _merge(p, outs, lses, z, x, mod, w, bn, rb):
    nb, r, _ = x.shape
    tm = bn * rb
    n_tok = nb * r

    def rows(width, cb=0):
        return pl.BlockSpec((bn, rb, width), lambda a, b, cb=cb: (a, b, cb))

    def modspec(j):
        return pl.BlockSpec((bn, 1, D_MODEL), lambda a, b, j=j: (a, 0, j))

    def full(arr):
        nd = arr.ndim
        return pl.BlockSpec(arr.shape, lambda a, b, nd=nd: (0,) * nd)

    weights = [w["w_ret_o"], w["w_att_o"], w["w_out"], w["b_out"], w["ln1_g"], w["ln1_b"], w["peer_wq"],
               w["peer_keys"]]
    nrb = r // rb
    x1, st = pl.pallas_call(
        _merge_kernel,
        grid=(nb // bn, nrb),
        in_specs=[rows(ATT_W)] * 7
                 + [rows(COL_BLK, CB_GRET), rows(COL_BLK, CB_GRET + 1),
                    rows(COL_BLK, CB_GATT), rows(COL_BLK, CB_GATT + 1),
                    rows(D_MODEL), modspec(2), modspec(3), modspec(4)]
                 + [full(a) for a in weights],
        out_specs=[rows(D_MODEL),
                   pl.BlockSpec((N_HS * N_KEYS, tm), lambda a, b: (0, a * nrb + b))],
        out_shape=[jax.ShapeDtypeStruct((nb, r, D_MODEL), F32),
                   jax.ShapeDtypeStruct((N_HS * N_KEYS, n_tok), F32)],
        compiler_params=_cparams(2, VMEM_LIMIT),
        name="merge",
    )(*outs, *lses, z, p, p, p, p, x, mod, mod, mod, *weights)
    return x1, st


TOPK_TOK = 128


def _top16(x, n, val_ref, idx_ref):
    row = lax.broadcasted_iota(jnp.int32, x.shape, 0).astype(F32)
    for r in range(TOPK):
        m = jnp.max(x, axis=0, keepdims=True)
        am = jnp.min(jnp.where(x == m, row, float(n)), axis=0, keepdims=True)
        val_ref[pl.ds(r, 1), :] = m
        idx_ref[pl.ds(r, 1), :] = am.astype(jnp.int32)
        x = jnp.where(row == am, -jnp.inf, x)


def _topk_kernel(st_ref, e_ref, g_ref, sv, si, cand, cv, ci):
    def stage1(hs, carry):
        x = st_ref[pl.ds(pl.multiple_of(hs * N_KEYS, N_KEYS), N_KEYS), :]
        _top16(x, N_KEYS, sv.at[hs], si.at[hs])
        return carry

    lax.fori_loop(0, N_HS, stage1, 0)

    def stage2(h, carry):
        a = sv[2 * h]
        b = sv[2 * h + 1]
        for i in range(TOPK):
            cand[i * TOPK:(i + 1) * TOPK, :] = a[i:i + 1, :] + b
        _top16(cand[...], TOPK * TOPK, cv, ci)
        c_val = cv[...]
        c_idx = ci[...]
        ia = lax.shift_right_logical(c_idx, 4)
        ib = c_idx & (TOPK - 1)
        sa = si[2 * h]
        sb = si[2 * h + 1]
        i0 = jnp.zeros_like(c_idx)
        i1 = jnp.zeros_like(c_idx)
        for i in range(TOPK):
            i0 = jnp.where(ia == i, sa[i:i + 1, :], i0)
            i1 = jnp.where(ib == i, sb[i:i + 1, :], i1)
        ex = jnp.exp(c_val - jnp.max(c_val, axis=0, keepdims=True))
        rows = pl.ds(pl.multiple_of(h * TOPK, TOPK), TOPK)
        e_ref[rows, :] = i0 * N_KEYS + i1
        g_ref[rows, :] = ex / jnp.sum(ex, axis=0, keepdims=True)
        return carry

    lax.fori_loop(0, H_PEER, stage2, 0)


def _topk(st):
    n_tok = st.shape[1]
    tt = TOPK_TOK
    spec = pl.BlockSpec((N_SEL, tt), lambda i: (0, i))
    return pl.pallas_call(
        _topk_kernel,
        grid=(n_tok // tt,),
        in_specs=[pl.BlockSpec((N_HS * N_KEYS, tt), lambda i: (0, i))],
        out_specs=[spec, spec],
        out_shape=[jax.ShapeDtypeStruct((N_SEL, n_tok), jnp.int32),
                   jax.ShapeDtypeStruct((N_SEL, n_tok), F32)],
        scratch_shapes=[pltpu.VMEM((N_HS, TOPK, tt), F32), pltpu.VMEM((N_HS, TOPK, tt), jnp.int32),
                        pltpu.VMEM((TOPK * TOPK, tt), F32),
                        pltpu.VMEM((TOPK, tt), F32), pltpu.VMEM((TOPK, tt), jnp.int32)],
        compiler_params=_cparams(1, VMEM_LIMIT),
        name="topk",
    )(st)


G_LANES = 128
PEER_ROWS = PEER_TOK * N_SEL


def _peer_kernel(e0_ref, en_ref, gt_ref, x1_ref, sh2_ref, sc2_ref, gt2_ref, lng, lnb, uv_hbm, o_ref,
                 buf, sem, y_scr):
    i = pl.program_id(0)
    n = pl.num_programs(0)
    slot = i % 2

    def issue(e_ref, to_slot):
        base = to_slot * PEER_ROWS

        def body(r, carry):
            for j in range(PEER_TOK):
                pltpu.make_async_copy(uv_hbm.at[e_ref[j, r]], buf.at[base + j * N_SEL + r],
                                      sem.at[to_slot]).start()
            return carry

        lax.fori_loop(0, N_SEL, body, 0)

    @pl.when(i == 0)
    def _():
        issue(e0_ref, 0)

    @pl.when(i + 1 < n)
    def _():
        issue(en_ref, 1 - slot)

    base = pl.multiple_of(slot * PEER_ROWS, PEER_ROWS)
    pltpu.make_async_copy(uv_hbm.at[pl.ds(0, PEER_ROWS)], buf.at[pl.ds(base, PEER_ROWS)], sem.at[slot]).wait()

    x1 = x1_ref[0]
    h2 = x1 * (1.0 + sc2_ref[0]) + sh2_ref[0]
    ones = jnp.ones((N_SEL, N_SEL), BF16)
    tok_row = lax.broadcasted_iota(jnp.int32, (G_LANES, N_SEL), 0)
    gt = gt_ref[...]
    for j in range(PEER_TOK):
        rows = pl.ds(pl.multiple_of(base + j * N_SEL, N_SEL), N_SEL)
        prod = buf[rows, 0:D_MODEL] * h2[j:j + 1, :]
        part = prod[:, 0:N_SEL]
        for c in range(1, D_MODEL // N_SEL):
            part = part + prod[:, c * N_SEL:(c + 1) * N_SEL]
        a = _dot_exact(part, ones)
        act = 0.5 * a * (1.0 + lax.erf(a * (2.0 ** -0.5)))
        pick = jnp.where(tok_row == (i % (G_LANES // PEER_TOK)) * PEER_TOK + j, 1.0, 0.0).astype(BF16)
        wgt = _dot_exact(gt, pick) * act
        pieces = []
        for c in range(D_MODEL // N_SEL):
            vals = buf[rows, D_MODEL + c * N_SEL:D_MODEL + (c + 1) * N_SEL]
            pieces.append(jnp.sum(vals * wgt, axis=0, keepdims=True))
        y_scr[j:j + 1, :] = jnp.concatenate(pieces, axis=-1)
    y = ALPHA * x1 + gt2_ref[0] * y_scr[...]
    o_ref[0] = _layer_norm(y, lng[...], lnb[...])


def _peer(e_tok, g_t, x1, mod, uv, ln_g, ln_b):
    nb, r, _ = x1.shape
    n_tok = nb * r
    steps = n_tok // PEER_TOK
    nrb = r // PEER_TOK
    per_gate = G_LANES // PEER_TOK

    def tile(i):
        return (i // nrb, i % nrb, 0)

    def modspec(j):
        return pl.BlockSpec((1, 1, D_MODEL), lambda i, j=j: (i // nrb, 0, j))

    vec = pl.BlockSpec((1, D_MODEL), lambda i: (0, 0))
    return pl.pallas_call(
        _peer_kernel,
        grid=(steps,),
        in_specs=[pl.BlockSpec((PEER_TOK, N_SEL), lambda i: (0, 0), memory_space=pltpu.SMEM),
                  pl.BlockSpec((PEER_TOK, N_SEL), lambda i: (jnp.minimum(i + 1, steps - 1), 0),
                               memory_space=pltpu.SMEM),
                  pl.BlockSpec((N_SEL, G_LANES), lambda i: (0, i // per_gate)),
                  pl.BlockSpec((1, PEER_TOK, D_MODEL), tile),
                  modspec(3), modspec(4), modspec(5), vec, vec,
                  pl.BlockSpec(memory_space=pl.ANY)],
        out_specs=pl.BlockSpec((1, PEER_TOK, D_MODEL), tile),
        out_shape=jax.ShapeDtypeStruct((nb, r, D_MODEL), F32),
        scratch_shapes=[pltpu.VMEM((2 * PEER_ROWS, 2 * D_MODEL), F32),
                        pltpu.SemaphoreType.DMA((2,)),
                        pltpu.VMEM((PEER_TOK, D_MODEL), F32)],
        compiler_params=_cparams(1, VMEM_LIMIT),
        name="peer",
    )(e_tok, e_tok, g_t, x1, mod, mod, mod, ln_g, ln_b, uv)


def _branch(x, mod, pos0, s0, caches, biases, w, uv, *, prompt):
    nb, r, _ = x.shape
    if prompt:
        bn, rb_proj, rb_merge, chunk, bn_ret = 1, 1024, 256, RET_CHUNK, 1
    else:
        bn, rb_proj, rb_merge, chunk, bn_ret = 32, r, r, math.gcd(r, RET_CHUNK), 8
    p = _inproj(x, mod, w["w_in"], bn if not prompt else 1, rb_proj)
    cos, sin = _rotary_tables(pos0, r)
    z, s_new = _retention(p, cos, sin, s0, bn_ret, chunk)
    outs, lses, new_bufs = [], [], []
    for g, (win, dil) in enumerate(ATT_GROUPS):
        if prompt:
            o, l = _attn_prompt(p, biases[g], g, dil)
            nw = min(win, r)
            kg = p[:, r - nw:, (CB_AK + g) * COL_BLK:(CB_AK + g + 1) * COL_BLK]
            vg = p[:, r - nw:, (CB_AV + g) * COL_BLK:(CB_AV + g + 1) * COL_BLK]
            nbuf = jnp.stack([kg, vg], axis=2).reshape(nb, nw, 2, H_ATT, DH_ATT)
        else:
            o, l, nc = _attn_sample(p, caches[g], biases[g], g, dil)
            nbuf = nc.reshape(nb, nc.shape[1], 2, H_ATT, DH_ATT)
        outs.append(o)
        lses.append(l)
        new_bufs.append(nbuf)
    x1, st = _merge(p, outs, lses, z, x, mod, w, bn, rb_merge)
    e_t, g_t = _topk(st)
    y = _peer(e_t.T, g_t, x1, mod, uv, w["ln2_g"], w["ln2_b"])
    return y, s_new, new_bufs


def kernel(x_prompt, x_sample, state_ret, cache_att_w128, cache_att_w512, cache_att_w2048, c_prompt, c_sample,
           w_ada, b_ada, w_in, w_ret_o, w_att_o, w_out, b_out, ln1_g, ln1_b, peer_wq, peer_keys, peer_u, peer_v,
           ln2_g, ln2_b, rel_bias):
    nbp, seq, _ = x_prompt.shape
    nbs, dec_seq, _ = x_sample.shape
    w = {
        "w_in": w_in[0].astype(BF16),
        "w_ret_o": w_ret_o[0].astype(BF16),
        "w_att_o": w_att_o[0].astype(BF16),
        "w_out": w_out[0].astype(BF16),
        "b_out": b_out[0].reshape(1, -1),
        "ln1_g": ln1_g[0].reshape(1, -1),
        "ln1_b": ln1_b[0].reshape(1, -1),
        "peer_wq": peer_wq[0].astype(BF16),
        "peer_keys": peer_keys[0].reshape(N_HS, N_KEYS, N_KEYS).astype(BF16),
        "ln2_g": ln2_g[0].reshape(1, -1),
        "ln2_b": ln2_b[0].reshape(1, -1),
    }
    uv = jnp.concatenate([peer_u[0], peer_v[0]], axis=1)
    mod = _ada(jnp.concatenate([c_prompt, c_sample], axis=0), w_ada[0], b_ada[0])
    mod_p = mod[:nbp].reshape(nbp, 1, -1)
    mod_s = mod[nbp:].reshape(nbs, 1, -1)

    caches = (cache_att_w128[0], cache_att_w512[0], cache_att_w2048[0])
    bias_p, bias_s = [], []
    for g, (win, dil) in enumerate(ATT_GROUPS):
        n_buf = caches[g].shape[1]
        n_pad = n_buf + ATT_SPAN
        idx = np.concatenate([_prompt_bucket_idx(dil), _sample_bucket_idx(n_buf, n_pad, dec_seq)], axis=1)
        tab = _bias_lookup(rel_bias[:, g * H_ATT:(g + 1) * H_ATT].T, jnp.asarray(idx))
        n_p = ATT_SPAN * 2 * ATT_SPAN
        bias_p.append(tab[:, :n_p].reshape(H_ATT, ATT_SPAN, 2 * ATT_SPAN))
        bias_s.append(tab[:, n_p:].reshape(H_ATT, dec_seq, n_pad))

    s0_p = jnp.zeros((nbp, H_RET, DK_RET, DK_RET), F32)
    ys, rs, bs = _branch(x_sample, mod_s, PAST_LEN, state_ret[0], caches, bias_s, w, uv, prompt=False)
    yp, rp, bp = _branch(x_prompt, mod_p, 0, s0_p, None, bias_p, w, uv, prompt=True)
    return (yp, ys, rp[None], bp[0][None], bp[1][None], bp[2][None],
            rs[None], bs[0][None], bs[1][None], bs[2][None])
```

```python
import functools
import math

import numpy as np
import jax
import jax.numpy as jnp
from jax import lax
from jax.experimental import pallas as pl
from jax.experimental.pallas import tpu as pltpu

F32 = jnp.float32
BF16 = jnp.bfloat16

D_MODEL = 1024
PAST_LEN = 8192
H_RET = 4
DK_RET = 128
RET_CHUNK = 128
ROPE_BASE = 10000.0
ATT_GROUPS = ((128, 1), (512, 4), (2048, 16))
N_GROUPS = 3
H_ATT = 8
DH_ATT = 64
ATT_SPAN = 128
ATT_W = H_ATT * DH_ATT
NEG = -1e30
N_BUCKETS = 32
MAX_DISTANCE = 2048
N_KEYS = 128
H_PEER = 8
TOPK = 16
N_HS = 2 * H_PEER
N_SEL = H_PEER * TOPK
ALPHA = 2.0 ** 0.25
LN_EPS = 1e-5
HN_EPS = 1e-6
IN_COLS = 8704
COL_BLK = 512
CB_RQ, CB_RK, CB_RV, CB_RG = 0, 1, 2, 3
CB_AQ, CB_AK, CB_AV = 4, 7, 10
CB_GRET, CB_GATT = 13, 15
N_CB = IN_COLS // COL_BLK
LOG_GAMMA = [math.log1p(-(2.0 ** (-5.0 - h))) for h in range(H_RET)]
PEER_TOK = 8
VMEM_LIMIT = 56 * 1024 * 1024


def _cparams(n_axes, vmem=None):
    return pltpu.CompilerParams(dimension_semantics=("arbitrary",) * n_axes,
                                vmem_limit_bytes=vmem)


def _dot(a, b):
    return jnp.dot(a, b, preferred_element_type=F32)


def _dot_nt(a, b):
    return lax.dot_general(a, b, (((1,), (1,)), ((), ())), preferred_element_type=F32)


def _dot_tn(a, b):
    return lax.dot_general(a, b, (((0,), (0,)), ((), ())), preferred_element_type=F32)


def _dot_exact(x, sel):
    hi = x.astype(BF16)
    r1 = x - hi.astype(F32)
    mid = r1.astype(BF16)
    lo = (r1 - mid.astype(F32)).astype(BF16)
    return _dot(hi, sel) + _dot(mid, sel) + _dot(lo, sel)


def _layer_norm(y, g, b):
    mu = jnp.mean(y, axis=-1, keepdims=True)
    d = y - mu
    var = jnp.mean(d * d, axis=-1, keepdims=True)
    return d * lax.rsqrt(var + LN_EPS) * g + b


def _ada_kernel(c_ref, w_ref, b_ref, o_ref):
    c = c_ref[...]
    s = c * jax.nn.sigmoid(c)
    o_ref[...] = _dot(s.astype(BF16), w_ref[...].astype(BF16)) + b_ref[...]


def _ada(c, w, b):
    n = c.shape[0]
    return pl.pallas_call(
        _ada_kernel,
        grid=(6,),
        in_specs=[pl.BlockSpec((n, D_MODEL), lambda j: (0, 0)),
                  pl.BlockSpec((D_MODEL, D_MODEL), lambda j: (0, j)),
                  pl.BlockSpec((1, D_MODEL), lambda j: (0, j))],
        out_specs=pl.BlockSpec((n, D_MODEL), lambda j: (0, j)),
        out_shape=jax.ShapeDtypeStruct((n, 6 * D_MODEL), F32),
        compiler_params=_cparams(1, VMEM_LIMIT),
        name="ada",
    )(c, w, b.reshape(1, -1))


def _inproj_kernel(x_ref, sh_ref, sc_ref, w_ref, o_ref, h_scr):
    @pl.when(pl.program_id(2) == 0)
    def _():
        h = x_ref[...] * (1.0 + sc_ref[...]) + sh_ref[...]
        h_scr[...] = h.reshape(h_scr.shape).astype(BF16)

    o_ref[...] = _dot(h_scr[...], w_ref[...]).reshape(o_ref.shape)


def _inproj(x, mod, w_b, bn, rb):
    nb, r, _ = x.shape
    return pl.pallas_call(
        _inproj_kernel,
        grid=(nb // bn, r // rb, N_CB),
        in_specs=[pl.BlockSpec((bn, rb, D_MODEL), lambda a, b, j: (a, b, 0)),
                  pl.BlockSpec((bn, 1, D_MODEL), lambda a, b, j: (a, 0, 0)),
                  pl.BlockSpec((bn, 1, D_MODEL), lambda a, b, j: (a, 0, 1)),
                  pl.BlockSpec((D_MODEL, COL_BLK), lambda a, b, j: (0, j))],
        out_specs=pl.BlockSpec((bn, rb, COL_BLK), lambda a, b, j: (a, b, j)),
        out_shape=jax.ShapeDtypeStruct((nb, r, IN_COLS), F32),
        scratch_shapes=[pltpu.VMEM((bn * rb, D_MODEL), BF16)],
        compiler_params=_cparams(3, VMEM_LIMIT),
        name="inproj",
    )(x, mod, mod, w_b)


def _inproj_res_kernel(x_ref, sh_ref, sc_ref, w_ref, o_ref, r1_ref, r2_ref, h_scr, res_scr):
    j = pl.program_id(2)

    @pl.when(j == 0)
    def _():
        h = x_ref[...] * (1.0 + sc_ref[...]) + sh_ref[...]
        h_scr[...] = h.reshape(h_scr.shape).astype(BF16)

    res = _dot(h_scr[...], w_ref[...])
    o_ref[0] = res
    rb = res_scr.shape[1]
    for c in range(COL_BLK // LANES):
        res_scr[c] = res[:, c * LANES:(c + 1) * LANES]
    for g, r_ref in ((1, r1_ref), (2, r2_ref)):
        dil = ATT_GROUPS[g][1]

        @pl.when((j == CB_AQ + g) | (j == CB_AK + g) | (j == CB_AV + g))
        def _(r_ref=r_ref, dil=dil):
            for r in range(dil):
                for c in range(COL_BLK // LANES):
                    r_ref[0, r, :, c * LANES:(c + 1) * LANES] = res_scr[c, pl.ds(r, rb // dil, stride=dil), :]


def _inproj_res(x, mod, w_b, rb):
    nb, r, _ = x.shape

    def res_spec(g):
        dil = ATT_GROUPS[g][1]
        return pl.BlockSpec(
            (1, dil, rb // dil, COL_BLK),
            lambda a, b, j, g=g: (a, 0, b, (j > CB_AQ + g).astype(jnp.int32) + (j > CB_AK + g).astype(jnp.int32)))

    def res_shape(g):
        dil = ATT_GROUPS[g][1]
        return jax.ShapeDtypeStruct((nb, dil, r // dil, 3 * COL_BLK), F32)

    return pl.pallas_call(
        _inproj_res_kernel,
        grid=(nb, r // rb, N_CB),
        in_specs=[pl.BlockSpec((1, rb, D_MODEL), lambda a, b, j: (a, b, 0)),
                  pl.BlockSpec((1, 1, D_MODEL), lambda a, b, j: (a, 0, 0)),
                  pl.BlockSpec((1, 1, D_MODEL), lambda a, b, j: (a, 0, 1)),
                  pl.BlockSpec((D_MODEL, COL_BLK), lambda a, b, j: (0, j))],
        out_specs=[pl.BlockSpec((1, rb, COL_BLK), lambda a, b, j: (a, b, j)), res_spec(1), res_spec(2)],
        out_shape=[jax.ShapeDtypeStruct((nb, r, IN_COLS), F32), res_shape(1), res_shape(2)],
        scratch_shapes=[pltpu.VMEM((rb, D_MODEL), BF16), pltpu.VMEM((COL_BLK // LANES, rb, LANES), F32)],
        compiler_params=_cparams(3, VMEM_LIMIT),
        name="inproj_res",
    )(x, mod, mod, w_b)


def _ret_kernel(q_ref, k_ref, v_ref, g_ref, cos_ref, sin_ref, s0_ref, z_ref, s_ref, *, bn, chunk):
    @pl.when(pl.program_id(1) == 0)
    def _():
        s_ref[...] = s0_ref[...]

    cos = cos_ref[...]
    sin = sin_ref[...]
    ri = lax.broadcasted_iota(jnp.int32, (chunk, chunk), 0)
    rj = lax.broadcasted_iota(jnp.int32, (chunk, chunk), 1)
    rel = (ri - rj).astype(F32)
    ci = lax.broadcasted_iota(jnp.int32, (chunk, 1), 0).astype(F32)
    for b in range(bn):
        for h in range(H_RET):
            sl = slice(h * DK_RET, (h + 1) * DK_RET)
            lg = LOG_GAMMA[h]
            q = q_ref[b, :, sl]
            k = k_ref[b, :, sl]
            v = v_ref[b, :, sl]
            qr = q * cos + pltpu.roll(q, DK_RET // 2, 1) * sin
            kr = (k * cos + pltpu.roll(k, DK_RET // 2, 1) * sin) * (DK_RET ** -0.5)
            intra = jnp.where(rel >= 0, jnp.exp(lg * jnp.maximum(rel, 0.0)), 0.0)
            scores = _dot_nt(qr, kr) * intra
            s = s_ref[b, h]
            o = _dot(scores, v) + _dot(qr, s) * jnp.exp(lg * (ci + 1.0))
            k_dec = jnp.exp(lg * (chunk - 1.0 - ci))
            s_ref[b, h] = s * math.exp(lg * chunk) + _dot_tn(kr * k_dec, v)
            mu = jnp.mean(o, axis=-1, keepdims=True)
            d = o - mu
            var = jnp.mean(d * d, axis=-1, keepdims=True)
            g = g_ref[b, :, sl]
            z_ref[b, :, sl] = g * jax.nn.sigmoid(g) * (d * lax.rsqrt(var + HN_EPS))


def _retention(p, cos, sin, s0, bn, chunk):
    nb, r, _ = p.shape

    def col(cb):
        return pl.BlockSpec((bn, chunk, COL_BLK), lambda a, c, cb=cb: (a, c, cb))

    st_spec = pl.BlockSpec((bn, H_RET, DK_RET, DK_RET), lambda a, c: (a, 0, 0, 0))
    return pl.pallas_call(
        functools.partial(_ret_kernel, bn=bn, chunk=chunk),
        grid=(nb // bn, r // chunk),
        in_specs=[col(CB_RQ), col(CB_RK), col(CB_RV), col(CB_RG),
                  pl.BlockSpec((chunk, DK_RET), lambda a, c: (c, 0)),
                  pl.BlockSpec((chunk, DK_RET), lambda a, c: (c, 0)),
                  st_spec],
        out_specs=[pl.BlockSpec((bn, chunk, COL_BLK), lambda a, c: (a, c, 0)), st_spec],
        out_shape=[jax.ShapeDtypeStruct((nb, r, COL_BLK), F32),
                   jax.ShapeDtypeStruct((nb, H_RET, DK_RET, DK_RET), F32)],
        compiler_params=_cparams(2, VMEM_LIMIT),
        name="retention",
    )(p, p, p, p, cos, sin, s0)


def _rotary_tables(pos0, length):
    half = DK_RET // 2
    inv = ROPE_BASE ** (-jnp.arange(half, dtype=F32) / half)
    ang = (pos0 + jnp.arange(length, dtype=jnp.int32)).astype(F32)[:, None] * inv[None, :]
    cos, sin = jnp.cos(ang), jnp.sin(ang)
    return jnp.concatenate([cos, cos], axis=-1), jnp.concatenate([-sin, sin], axis=-1)


def _t5_bucket_np(dist):
    max_exact = N_BUCKETS // 2
    d = dist.astype(np.float32)
    large = np.float32(max_exact) + (np.log(np.maximum(d, np.float32(1.0)) / np.float32(max_exact))
                                     / np.float32(math.log(MAX_DISTANCE / max_exact))
                                     * np.float32(N_BUCKETS - max_exact))
    large = np.minimum(large.astype(np.int32), N_BUCKETS - 1)
    return np.where(dist < max_exact, dist, large).astype(np.int32)


BIAS_CHUNK = 1024


def _bias_kernel(tab_ref, idx_ref, o_ref):
    idx = idx_ref[...]
    acc = jnp.zeros(o_ref.shape, F32)
    for b in range(N_BUCKETS):
        acc = jnp.where(idx == b, tab_ref[:, b:b + 1], acc)
    o_ref[...] = acc


def _bias_lookup(tab_t, idx):
    n = idx.shape[1]
    return pl.pallas_call(
        _bias_kernel,
        grid=(n // BIAS_CHUNK,),
        in_specs=[pl.BlockSpec((H_ATT, N_BUCKETS), lambda j: (0, 0)),
                  pl.BlockSpec((1, BIAS_CHUNK), lambda j: (0, j))],
        out_specs=pl.BlockSpec((H_ATT, BIAS_CHUNK), lambda j: (0, j)),
        out_shape=jax.ShapeDtypeStruct((H_ATT, n), F32),
        compiler_params=_cparams(1),
        name="bias_lookup",
    )(tab_t, idx)


def _prompt_bucket_idx(dil):
    a = np.arange(ATT_SPAN)[:, None]
    b = np.arange(2 * ATT_SPAN)[None, :]
    rel = a + ATT_SPAN - b
    return _t5_bucket_np(np.clip(rel, 0, None) * dil).reshape(1, -1)


def _sample_bucket_idx(n_buf, n_pad, t_len):
    t = np.arange(t_len)[:, None]
    i = np.arange(n_pad)[None, :]
    return _t5_bucket_np(np.clip(n_buf + t - i, 0, None)).reshape(1, -1)


def _attn_p_kernel(q_ref, kp_ref, kc_ref, vp_ref, vc_ref, bias_ref, o_ref, l_ref):
    blk = ATT_SPAN
    first = pl.program_id(2) == 0
    a = lax.broadcasted_iota(jnp.int32, (blk, 2 * blk), 0)
    b = lax.broadcasted_iota(jnp.int32, (blk, 2 * blk), 1)
    rel = a + blk - b
    k_lo = jnp.where(first, blk, 0)
    valid = (rel >= 0) & (rel <= ATT_SPAN) & (b >= k_lo)
    for h in range(H_ATT):
        sl = slice(h * DH_ATT, (h + 1) * DH_ATT)
        q = q_ref[0, 0, :, sl].astype(BF16)
        k = jnp.concatenate([kp_ref[0, 0, :, sl], kc_ref[0, 0, :, sl]], axis=0).astype(BF16)
        v = jnp.concatenate([vp_ref[0, 0, :, sl], vc_ref[0, 0, :, sl]], axis=0).astype(BF16)
        logits = _dot_nt(q, k) * (DH_ATT ** -0.5) + bias_ref[h]
        logits = jnp.where(valid, logits, NEG)
        m = jnp.max(logits, axis=-1, keepdims=True)
        p = jnp.exp(logits - m)
        l = jnp.sum(p, axis=-1, keepdims=True)
        o_ref[0, 0, :, sl] = _dot(p.astype(BF16), v) / l
        l_ref[0, 0, :, sl] = jnp.broadcast_to(m + jnp.log(l), (blk, DH_ATT))


def _attn_prompt(src, bias, cbs):
    nb, dil, n, _ = src.shape
    cb_q, cb_k, cb_v = cbs

    def cur(cb):
        return pl.BlockSpec((1, 1, ATT_SPAN, COL_BLK), lambda bb, r, i, cb=cb: (bb, r, i, cb))

    def prev(cb):
        return pl.BlockSpec((1, 1, ATT_SPAN, COL_BLK),
                            lambda bb, r, i, cb=cb: (bb, r, jnp.maximum(i - 1, 0), cb))

    out_spec = pl.BlockSpec((1, 1, ATT_SPAN, ATT_W), lambda bb, r, i: (bb, r, i, 0))
    return pl.pallas_call(
        _attn_p_kernel,
        grid=(nb, dil, n // ATT_SPAN),
        in_specs=[cur(cb_q), prev(cb_k), cur(cb_k), prev(cb_v), cur(cb_v),
                  pl.BlockSpec((H_ATT, ATT_SPAN, 2 * ATT_SPAN), lambda bb, r, i: (0, 0, 0))],
        out_specs=[out_spec, out_spec],
        out_shape=[jax.ShapeDtypeStruct((nb, dil, n, ATT_W), F32)] * 2,
        compiler_params=_cparams(3, VMEM_LIMIT),
        name="attn_prompt",
    )(src, src, src, src, src, bias)


def _attn_s_kernel(q_ref, k_ref, v_ref, c_ref, bias_ref, o_ref, l_ref, nc_ref, kall, vall, *, n_buf, dil):
    t_len = q_ref.shape[1]
    n_pad = kall.shape[0]

    @pl.when(pl.program_id(0) == 0)
    def _():
        kall[n_buf + t_len:, :] = jnp.zeros((n_pad - n_buf - t_len, ATT_W), F32)
        vall[n_buf + t_len:, :] = jnp.zeros((n_pad - n_buf - t_len, ATT_W), F32)

    kall[0:n_buf, :] = c_ref[0, :, 0:ATT_W]
    vall[0:n_buf, :] = c_ref[0, :, ATT_W:2 * ATT_W]
    kall[n_buf:n_buf + t_len, :] = k_ref[0]
    vall[n_buf:n_buf + t_len, :] = v_ref[0]
    nc_ref[0, :, 0:ATT_W] = kall[t_len:n_buf + t_len, :]
    nc_ref[0, :, ATT_W:2 * ATT_W] = vall[t_len:n_buf + t_len, :]

    t = lax.broadcasted_iota(jnp.int32, (t_len, n_pad), 0)
    i = lax.broadcasted_iota(jnp.int32, (t_len, n_pad), 1)
    d = n_buf + t - i
    valid = (d >= 0) & ((d & (dil - 1)) == 0) & (d <= ATT_SPAN * dil)
    for h in range(H_ATT):
        sl = slice(h * DH_ATT, (h + 1) * DH_ATT)
        q = q_ref[0, :, sl].astype(BF16)
        k = kall[:, sl].astype(BF16)
        v = vall[:, sl].astype(BF16)
        logits = _dot_nt(q, k) * (DH_ATT ** -0.5) + bias_ref[h]
        logits = jnp.where(valid, logits, NEG)
        m = jnp.max(logits, axis=-1, keepdims=True)
        p = jnp.exp(logits - m)
        l = jnp.sum(p, axis=-1, keepdims=True)
        o_ref[0, :, sl] = _dot(p.astype(BF16), v) / l
        l_ref[0, :, sl] = jnp.broadcast_to(m + jnp.log(l), (t_len, DH_ATT))


def _attn_sample(p, cache, bias, g, dil):
    nb, t_len, _ = p.shape
    n_buf = cache.shape[1]
    n_pad = bias.shape[2]
    cv = cache.reshape(nb, n_buf, 2 * ATT_W)

    def col(cb):
        return pl.BlockSpec((1, t_len, COL_BLK), lambda n, cb=cb: (n, 0, cb))

    small = pl.BlockSpec((1, t_len, ATT_W), lambda n: (n, 0, 0))
    big = pl.BlockSpec((1, n_buf, 2 * ATT_W), lambda n: (n, 0, 0))
    o, l, nc = pl.pallas_call(
        functools.partial(_attn_s_kernel, n_buf=n_buf, dil=dil),
        grid=(nb,),
        in_specs=[col(CB_AQ + g), col(CB_AK + g), col(CB_AV + g), big,
                  pl.BlockSpec((H_ATT, t_len, n_pad), lambda n: (0, 0, 0))],
        out_specs=[small, small, big],
        out_shape=[jax.ShapeDtypeStruct((nb, t_len, ATT_W), F32)] * 2
                  + [jax.ShapeDtypeStruct((nb, n_buf, 2 * ATT_W), F32)],
        scratch_shapes=[pltpu.VMEM((n_pad, ATT_W), F32), pltpu.VMEM((n_pad, ATT_W), F32)],
        compiler_params=_cparams(1, VMEM_LIMIT),
        name="attn_sample",
    )(p, p, p, cv, bias)
    return o, l, nc


def _merge_kernel(o0, o1, o2, l0, l1, l2, z_ref, gra, grb, gaa, gab, x_ref, gt1_ref, sh2_ref, sc2_ref,
                  wro, wao, wout, bout, lng, lnb, wq, keys, x1_ref, st_ref, il_scr):
    tm = st_ref.shape[1]

    def flat(ref):
        if len(ref.shape) == 4:
            dil = ref.shape[1]
            if dil == 1:
                return ref[0, 0]
            n_c = ATT_W // LANES
            for r in range(dil):
                for c in range(n_c):
                    il_scr[c, pl.ds(r, tm // dil, stride=dil), :] = ref[0, r, :, c * LANES:(c + 1) * LANES]
            return jnp.concatenate([il_scr[c] for c in range(n_c)], axis=-1)
        v = ref[...]
        return v.reshape(tm, v.shape[-1])

    la, lb, lc = flat(l0), flat(l1), flat(l2)
    m = jnp.maximum(jnp.maximum(la, lb), lc)
    ea, eb, ec = jnp.exp(la - m), jnp.exp(lb - m), jnp.exp(lc - m)
    att = (ea * flat(o0) + eb * flat(o1) + ec * flat(o2)) / (ea + eb + ec)
    att_y = _dot(att.astype(BF16), wao[...])
    ret_y = _dot(flat(z_ref).astype(BF16), wro[...])
    g_ret = jnp.concatenate([flat(gra), flat(grb)], axis=-1)
    g_att = jnp.concatenate([flat(gaa), flat(gab)], axis=-1)
    zz = jax.nn.sigmoid(g_ret) * ret_y + jax.nn.sigmoid(g_att) * att_y
    mix = _dot(zz.astype(BF16), wout[...]) + bout[...]
    x = x_ref[...]
    y = ALPHA * x + gt1_ref[...] * mix.reshape(x.shape)
    x1 = _layer_norm(y, lng[...], lnb[...])
    x1_ref[...] = x1
    h2 = (x1 * (1.0 + sc2_ref[...]) + sh2_ref[...]).reshape(tm, D_MODEL)
    q = _dot(h2.astype(BF16), wq[...]).astype(BF16)
    for hs in range(N_HS):
        st_ref[hs * N_KEYS:(hs + 1) * N_KEYS, :] = _dot_nt(keys[hs], q[:, hs * N_KEYS:(hs + 1) * N_KEYS])


def _merge(p, outs, lses, z, x, mod, w, bn, rb):
    nb, r, _ = x.shape
    tm = bn * rb
    n_tok = nb * r

    def rows(width, cb=0):
        return pl.BlockSpec((bn, rb, width), lambda a, b, cb=cb: (a, b, cb))

    def modspec(j):
        return pl.BlockSpec((bn, 1, D_MODEL), lambda a, b, j=j: (a, 0, j))

    def full(arr):
        nd = arr.ndim
        return pl.BlockSpec(arr.shape, lambda a, b, nd=nd: (0,) * nd)

    weights = [w["w_ret_o"], w["w_att_o"], w["w_out"], w["b_out"], w["ln1_g"], w["ln1_b"], w["peer_wq"],
               w["peer_keys"]]
    nrb = r // rb

    def group_spec(arr):
        if arr.ndim == 3:
            return rows(ATT_W)
        dil = arr.shape[1]
        return pl.BlockSpec((1, dil, tm // dil, ATT_W), lambda a, b: (a, 0, b, 0))

    x1, st = pl.pallas_call(
        _merge_kernel,
        grid=(nb // bn, nrb),
        in_specs=[group_spec(a) for a in (*outs, *lses)] + [rows(ATT_W)]
                 + [rows(COL_BLK, CB_GRET), rows(COL_BLK, CB_GRET + 1),
                    rows(COL_BLK, CB_GATT), rows(COL_BLK, CB_GATT + 1),
                    rows(D_MODEL), modspec(2), modspec(3), modspec(4)]
                 + [full(a) for a in weights],
        out_specs=[rows(D_MODEL),
                   pl.BlockSpec((N_HS * N_KEYS, tm), lambda a, b: (0, a * nrb + b))],
        out_shape=[jax.ShapeDtypeStruct((nb, r, D_MODEL), F32),
                   jax.ShapeDtypeStruct((N_HS * N_KEYS, n_tok), F32)],
        scratch_shapes=[pltpu.VMEM((ATT_W // LANES, tm, LANES), F32)],
        compiler_params=_cparams(2, VMEM_LIMIT),
        name="merge",
    )(*outs, *lses, z, p, p, p, p, x, mod, mod, mod, *weights)
    return x1, st


TOPK_TOK = 128
SUB = 8
CAND_ROWS = [TOPK // (i + 1) for i in range(TOPK)]
N_CAND = sum(CAND_ROWS)
N_CAND_PAD = -(-N_CAND // SUB) * SUB
CAND_PAD_IDX = TOPK * TOPK


def _cand_flat_index():
    flat = [i * TOPK + j for i in range(TOPK) for j in range(CAND_ROWS[i])]
    flat += [CAND_PAD_IDX] * (N_CAND_PAD - N_CAND)
    return np.broadcast_to(np.asarray(flat, np.int32)[:, None], (N_CAND_PAD, TOPK_TOK)).copy()


def _argmax_cols(xs, idxs):
    while len(xs) > 1:
        nx, ni = [], []
        for k in range(0, len(xs) - 1, 2):
            nx.append(jnp.maximum(xs[k], xs[k + 1]))
            ni.append(jnp.where(xs[k] >= xs[k + 1], idxs[k], idxs[k + 1]))
        if len(xs) % 2:
            nx.append(xs[-1])
            ni.append(idxs[-1])
        xs, idxs = nx, ni
    v, i = xs[0], idxs[0]
    for shift in (4, 2, 1):
        v2 = pltpu.roll(v, shift, 0)
        i2 = pltpu.roll(i, shift, 0)
        take = (v > v2) | ((v == v2) & (i < i2))
        v = jnp.where(take, v, v2)
        i = jnp.where(take, i, i2)
    return v, i


def _select16(xs, idxs, val_ref, idx_ref):
    xs = list(xs)
    for r in range(TOPK):
        v, i = _argmax_cols(xs, idxs)
        val_ref[pl.ds(r, 1), :] = v[0:1]
        idx_ref[pl.ds(r, 1), :] = i[0:1]
        xs = [jnp.where(ik == i, -jnp.inf, xk) for xk, ik in zip(xs, idxs)]


def _topk_kernel(st_ref, cidx_ref, e_ref, g_ref, sv, si, cand, cv, ci):
    tok = st_ref.shape[1]
    sub_iota = lax.broadcasted_iota(jnp.int32, (SUB, tok), 0)
    key_idx = [sub_iota + SUB * k for k in range(N_KEYS // SUB)]
    cand_idx = [cidx_ref[SUB * k:SUB * (k + 1), :] for k in range(N_CAND_PAD // SUB)]

    def head(h, carry):
        for half in range(2):
            base = pl.multiple_of((2 * h + half) * N_KEYS, N_KEYS)
            xs = [st_ref[pl.ds(base + SUB * k, SUB), :] for k in range(N_KEYS // SUB)]
            _select16(xs, key_idx, sv.at[half], si.at[half])
        a = sv[0]
        b = sv[1]
        cand[N_CAND_PAD - SUB:, :] = jnp.full((SUB, tok), -jnp.inf, F32)
        off = 0
        for i in range(TOPK):
            cand[off:off + CAND_ROWS[i], :] = a[i:i + 1, :] + b[0:CAND_ROWS[i], :]
            off += CAND_ROWS[i]
        _select16([cand[SUB * k:SUB * (k + 1), :] for k in range(N_CAND_PAD // SUB)], cand_idx, cv, ci)
        c_val = cv[...]
        c_idx = ci[...]
        ia = lax.shift_right_logical(c_idx, 4)
        ib = c_idx & (TOPK - 1)
        sa = si[0]
        sb = si[1]
        i0 = jnp.zeros_like(c_idx)
        i1 = jnp.zeros_like(c_idx)
        for i in range(TOPK):
            i0 = jnp.where(ia == i, sa[i:i + 1, :], i0)
            i1 = jnp.where(ib == i, sb[i:i + 1, :], i1)
        ex = jnp.exp(c_val - c_val[0:1, :])
        rows = pl.ds(pl.multiple_of(h * TOPK, TOPK), TOPK)
        e_ref[rows, :] = i0 * N_KEYS + i1
        g_ref[rows, :] = ex / jnp.sum(ex, axis=0, keepdims=True)
        return carry

    lax.fori_loop(0, H_PEER, head, 0)


def _topk(st):
    n_tok = st.shape[1]
    tt = TOPK_TOK
    spec = pl.BlockSpec((N_SEL, tt), lambda i: (0, i))
    return pl.pallas_call(
        _topk_kernel,
        grid=(n_tok // tt,),
        in_specs=[pl.BlockSpec((N_HS * N_KEYS, tt), lambda i: (0, i)),
                  pl.BlockSpec((N_CAND_PAD, tt), lambda i: (0, 0))],
        out_specs=[spec, spec],
        out_shape=[jax.ShapeDtypeStruct((N_SEL, n_tok), jnp.int32),
                   jax.ShapeDtypeStruct((N_SEL, n_tok), F32)],
        scratch_shapes=[pltpu.VMEM((2, TOPK, tt), F32), pltpu.VMEM((2, TOPK, tt), jnp.int32),
                        pltpu.VMEM((N_CAND_PAD, tt), F32),
                        pltpu.VMEM((TOPK, tt), F32), pltpu.VMEM((TOPK, tt), jnp.int32)],
        compiler_params=_cparams(1, VMEM_LIMIT),
        name="topk",
    )(st, jnp.asarray(_cand_flat_index()))


G_LANES = 128
LANES = 128
ROW_GRP = 8
N_GRP = N_SEL // ROW_GRP
N_CHUNK = D_MODEL // LANES
SLOT_GRPS = PEER_TOK * N_GRP
SLOT_ROWS = PEER_TOK * N_SEL
HI_MASK = -65536


def _pack_expert_table(u, v):
    ub = lax.bitcast_convert_type(u.astype(BF16), jnp.uint16).astype(jnp.uint32)
    vb = lax.bitcast_convert_type(v.astype(BF16), jnp.uint16).astype(jnp.uint32)
    word = lax.bitcast_convert_type((ub << 16) | vb, jnp.int32)
    return word.reshape(u.shape[0], N_CHUNK, LANES)


def _peer_kernel(e0_ref, en_ref, gt_ref, x1_ref, sh2_ref, sc2_ref, gt2_ref, lng, lnb, tab_hbm, o_ref,
                 buf, sem, y_scr):
    i = pl.program_id(0)
    n = pl.num_programs(0)
    slot = i % 2
    base = slot * SLOT_GRPS
    nxt = (1 - slot) * SLOT_GRPS

    def row_copy(e_ref, row, grp, s, to_slot):
        return pltpu.make_async_copy(tab_hbm.at[e_ref[row]], buf.at[grp, :, s, :], sem.at[to_slot])

    def wait_slot(which):
        span = buf.at[pl.ds(which * SLOT_GRPS, SLOT_GRPS)]
        pltpu.make_async_copy(span, span, sem.at[which]).wait()

    @pl.when(i == 0)
    def _():
        def body(q, carry):
            for s in range(ROW_GRP):
                row_copy(e0_ref, q * ROW_GRP + s, q, s, 0).start(priority=s % 2)
            return carry

        lax.fori_loop(0, SLOT_GRPS, body, 0)

    wait_slot(slot)

    x1 = x1_ref[0]
    h2 = x1 * (1.0 + sc2_ref[0]) + sh2_ref[0]
    ones = jnp.ones((N_SEL, N_SEL), BF16)
    tok_row = lax.broadcasted_iota(jnp.int32, (G_LANES, N_SEL), 0)
    gt = gt_ref[...]
    for j in range(PEER_TOK):
        xb = [jnp.broadcast_to(h2[j:j + 1, c * LANES:(c + 1) * LANES], (ROW_GRP, LANES)) for c in range(N_CHUNK)]
        parts = []
        for g in range(N_GRP):
            q = j * N_GRP + g
            for s in range(ROW_GRP):
                row_copy(en_ref, q * ROW_GRP + s, nxt + q, s, 1 - slot).start(priority=s % 2)
            words = buf[base + q]
            part = None
            for c in range(N_CHUNK):
                t = lax.bitcast_convert_type(words[c] & HI_MASK, F32) * xb[c]
                part = t if part is None else part + t
            parts.append(part)
        a = _dot_exact(jnp.concatenate(parts, axis=0), ones)
        act = 0.5 * a * (1.0 + lax.erf(a * (2.0 ** -0.5)))
        pick = jnp.where(tok_row == (i % (G_LANES // PEER_TOK)) * PEER_TOK + j, 1.0, 0.0).astype(BF16)
        wgt = _dot_exact(gt, pick) * act
        accs = [None] * N_CHUNK
        for g in range(N_GRP):
            words = buf[base + j * N_GRP + g]
            wg = wgt[g * ROW_GRP:(g + 1) * ROW_GRP, :]
            for c in range(N_CHUNK):
                t = lax.bitcast_convert_type(words[c] << 16, F32) * wg
                accs[c] = t if accs[c] is None else accs[c] + t
        y_scr[j:j + 1, :] = jnp.concatenate([jnp.sum(acc, axis=0, keepdims=True) for acc in accs], axis=-1)
    y = ALPHA * x1 + gt2_ref[0] * y_scr[...]
    o_ref[0] = _layer_norm(y, lng[...], lnb[...])

    @pl.when(i == n - 1)
    def _():
        wait_slot(1 - slot)


def _peer(e_tok, g_t, x1, mod, uv, ln_g, ln_b):
    nb, r, _ = x1.shape
    n_tok = nb * r
    steps = n_tok // PEER_TOK
    nrb = r // PEER_TOK
    per_gate = G_LANES // PEER_TOK

    def tile(i):
        return (i // nrb, i % nrb, 0)

    def modspec(j):
        return pl.BlockSpec((1, 1, D_MODEL), lambda i, j=j: (i // nrb, 0, j))

    vec = pl.BlockSpec((1, D_MODEL), lambda i: (0, 0))
    return pl.pallas_call(
        _peer_kernel,
        grid=(steps,),
        in_specs=[pl.BlockSpec((SLOT_ROWS,), lambda i: (0,), memory_space=pltpu.SMEM),
                  pl.BlockSpec((SLOT_ROWS,), lambda i: (jnp.minimum(i + 1, steps - 1),),
                               memory_space=pltpu.SMEM),
                  pl.BlockSpec((N_SEL, G_LANES), lambda i: (0, i // per_gate)),
                  pl.BlockSpec((1, PEER_TOK, D_MODEL), tile),
                  modspec(3), modspec(4), modspec(5), vec, vec,
                  pl.BlockSpec(memory_space=pl.ANY)],
        out_specs=pl.BlockSpec((1, PEER_TOK, D_MODEL), tile),
        out_shape=jax.ShapeDtypeStruct((nb, r, D_MODEL), F32),
        scratch_shapes=[pltpu.VMEM((2 * SLOT_GRPS, N_CHUNK, ROW_GRP, LANES), jnp.int32),
                        pltpu.SemaphoreType.DMA((2,)),
                        pltpu.VMEM((PEER_TOK, D_MODEL), F32)],
        compiler_params=_cparams(1, VMEM_LIMIT),
        name="peer",
    )(e_tok, e_tok, g_t, x1, mod, mod, mod, ln_g, ln_b, uv)


def _branch(x, mod, pos0, s0, caches, biases, w, uv, *, prompt):
    nb, r, _ = x.shape
    if prompt:
        bn, rb_proj, rb_merge, chunk, bn_ret = 1, 1024, 256, RET_CHUNK, 1
    else:
        bn, rb_proj, rb_merge, chunk, bn_ret = 32, r, r, math.gcd(r, RET_CHUNK), 8
    if prompt:
        p, res1, res2 = _inproj_res(x, mod, w["w_in"], rb_proj)
        att_src = ((p.reshape(nb, 1, r, IN_COLS), (CB_AQ, CB_AK, CB_AV)), (res1, (0, 1, 2)), (res2, (0, 1, 2)))
    else:
        p = _inproj(x, mod, w["w_in"], bn, rb_proj)
    cos, sin = _rotary_tables(pos0, r)
    z, s_new = _retention(p, cos, sin, s0, bn_ret, chunk)
    outs, lses, new_bufs = [], [], []
    for g, (win, dil) in enumerate(ATT_GROUPS):
        if prompt:
            o, l = _attn_prompt(att_src[g][0], biases[g], att_src[g][1])
            nw = min(win, r)
            kg = p[:, r - nw:, (CB_AK + g) * COL_BLK:(CB_AK + g + 1) * COL_BLK]
            vg = p[:, r - nw:, (CB_AV + g) * COL_BLK:(CB_AV + g + 1) * COL_BLK]
            nbuf = jnp.stack([kg, vg], axis=2).reshape(nb, nw, 2, H_ATT, DH_ATT)
        else:
            o, l, nc = _attn_sample(p, caches[g], biases[g], g, dil)
            nbuf = nc.reshape(nb, nc.shape[1], 2, H_ATT, DH_ATT)
        outs.append(o)
        lses.append(l)
        new_bufs.append(nbuf)
    x1, st = _merge(p, outs, lses, z, x, mod, w, bn, rb_merge)
    e_t, g_t = _topk(st)
    y = _peer(e_t.T.reshape(-1), g_t, x1, mod, uv, w["ln2_g"], w["ln2_b"])
    return y, s_new, new_bufs


def kernel(x_prompt, x_sample, state_ret, cache_att_w128, cache_att_w512, cache_att_w2048, c_prompt, c_sample,
           w_ada, b_ada, w_in, w_ret_o, w_att_o, w_out, b_out, ln1_g, ln1_b, peer_wq, peer_keys, peer_u, peer_v,
           ln2_g, ln2_b, rel_bias):
    nbp, seq, _ = x_prompt.shape
    nbs, dec_seq, _ = x_sample.shape
    w = {
        "w_in": w_in[0].astype(BF16),
        "w_ret_o": w_ret_o[0].astype(BF16),
        "w_att_o": w_att_o[0].astype(BF16),
        "w_out": w_out[0].astype(BF16),
        "b_out": b_out[0].reshape(1, -1),
        "ln1_g": ln1_g[0].reshape(1, -1),
        "ln1_b": ln1_b[0].reshape(1, -1),
        "peer_wq": peer_wq[0].astype(BF16),
        "peer_keys": peer_keys[0].reshape(N_HS, N_KEYS, N_KEYS).astype(BF16),
        "ln2_g": ln2_g[0].reshape(1, -1),
        "ln2_b": ln2_b[0].reshape(1, -1),
    }
    uv = _pack_expert_table(peer_u[0], peer_v[0])
    mod = _ada(jnp.concatenate([c_prompt, c_sample], axis=0), w_ada[0], b_ada[0])
    mod_p = mod[:nbp].reshape(nbp, 1, -1)
    mod_s = mod[nbp:].reshape(nbs, 1, -1)

    caches = (cache_att_w128[0], cache_att_w512[0], cache_att_w2048[0])
    bias_p, bias_s = [], []
    for g, (win, dil) in enumerate(ATT_GROUPS):
        n_buf = caches[g].shape[1]
        n_pad = n_buf + ATT_SPAN
        idx = np.concatenate([_prompt_bucket_idx(dil), _sample_bucket_idx(n_buf, n_pad, dec_seq)], axis=1)
        tab = _bias_lookup(rel_bias[:, g * H_ATT:(g + 1) * H_ATT].T, jnp.asarray(idx))
        n_p = ATT_SPAN * 2 * ATT_SPAN
        bias_p.append(tab[:, :n_p].reshape(H_ATT, ATT_SPAN, 2 * ATT_SPAN))
        bias_s.append(tab[:, n_p:].reshape(H_ATT, dec_seq, n_pad))

    s0_p = jnp.zeros((nbp, H_RET, DK_RET, DK_RET), F32)
    ys, rs, bs = _branch(x_sample, mod_s, PAST_LEN, state_ret[0], caches, bias_s, w, uv, prompt=False)
    yp, rp, bp = _branch(x_prompt, mod_p, 0, s0_p, None, bias_p, w, uv, prompt=True)
    return (yp, ys, rp[None], bp[0][None], bp[1][None], bp[2][None],
            rs[None], bs[0][None], bs[1][None], bs[2][None])
```

```python
import functools
import math

import numpy as np
import jax
import jax.numpy as jnp
from jax import lax
from jax.experimental import pallas as pl
from jax.experimental.pallas import tpu as pltpu

F32 = jnp.float32
BF16 = jnp.bfloat16

D_MODEL = 1024
PAST_LEN = 8192
H_RET = 4
DK_RET = 128
RET_CHUNK = 128
ROPE_BASE = 10000.0
ATT_GROUPS = ((128, 1), (512, 4), (2048, 16))
N_GROUPS = 3
H_ATT = 8
DH_ATT = 64
ATT_SPAN = 128
ATT_W = H_ATT * DH_ATT
NEG = -1e30
N_BUCKETS = 32
MAX_DISTANCE = 2048
N_KEYS = 128
H_PEER = 8
TOPK = 16
N_HS = 2 * H_PEER
N_SEL = H_PEER * TOPK
ALPHA = 2.0 ** 0.25
LN_EPS = 1e-5
HN_EPS = 1e-6
IN_COLS = 8704
COL_BLK = 512
CB_RQ, CB_RK, CB_RV, CB_RG = 0, 1, 2, 3
CB_AQ, CB_AK, CB_AV = 4, 7, 10
CB_GRET, CB_GATT = 13, 15
N_CB = IN_COLS // COL_BLK
LOG_GAMMA = [math.log1p(-(2.0 ** (-5.0 - h))) for h in range(H_RET)]
VMEM_LIMIT = 56 * 1024 * 1024


def _cparams(n_axes, vmem=None):
    return pltpu.CompilerParams(dimension_semantics=("arbitrary",) * n_axes,
                                vmem_limit_bytes=vmem)


def _dot(a, b):
    return jnp.dot(a, b, preferred_element_type=F32)


def _dot_nt(a, b):
    return lax.dot_general(a, b, (((1,), (1,)), ((), ())), preferred_element_type=F32)


def _dot_tn(a, b):
    return lax.dot_general(a, b, (((0,), (0,)), ((), ())), preferred_element_type=F32)


def _dot_exact(x, sel):
    hi = x.astype(BF16)
    r1 = x - hi.astype(F32)
    mid = r1.astype(BF16)
    lo = (r1 - mid.astype(F32)).astype(BF16)
    return _dot(hi, sel) + _dot(mid, sel) + _dot(lo, sel)


def _layer_norm(y, g, b):
    mu = jnp.mean(y, axis=-1, keepdims=True)
    d = y - mu
    var = jnp.mean(d * d, axis=-1, keepdims=True)
    return d * lax.rsqrt(var + LN_EPS) * g + b


def _ada_kernel(c_ref, w_ref, b_ref, o_ref):
    c = c_ref[...]
    s = c * jax.nn.sigmoid(c)
    o_ref[...] = _dot(s.astype(BF16), w_ref[...].astype(BF16)) + b_ref[...]


def _ada(c, w, b):
    n = c.shape[0]
    return pl.pallas_call(
        _ada_kernel,
        grid=(6,),
        in_specs=[pl.BlockSpec((n, D_MODEL), lambda j: (0, 0)),
                  pl.BlockSpec((D_MODEL, D_MODEL), lambda j: (0, j)),
                  pl.BlockSpec((1, D_MODEL), lambda j: (0, j))],
        out_specs=pl.BlockSpec((n, D_MODEL), lambda j: (0, j)),
        out_shape=jax.ShapeDtypeStruct((n, 6 * D_MODEL), F32),
        compiler_params=_cparams(1, VMEM_LIMIT),
        name="ada",
    )(c, w, b.reshape(1, -1))


def _inproj_kernel(x_ref, sh_ref, sc_ref, w_ref, o_ref, h_scr):
    @pl.when(pl.program_id(2) == 0)
    def _():
        h = x_ref[...] * (1.0 + sc_ref[...]) + sh_ref[...]
        h_scr[...] = h.reshape(h_scr.shape).astype(BF16)

    o_ref[...] = _dot(h_scr[...], w_ref[...]).reshape(o_ref.shape)


def _inproj(x, mod, w_b, bn, rb):
    nb, r, _ = x.shape
    return pl.pallas_call(
        _inproj_kernel,
        grid=(nb // bn, r // rb, N_CB),
        in_specs=[pl.BlockSpec((bn, rb, D_MODEL), lambda a, b, j: (a, b, 0)),
                  pl.BlockSpec((bn, 1, D_MODEL), lambda a, b, j: (a, 0, 0)),
                  pl.BlockSpec((bn, 1, D_MODEL), lambda a, b, j: (a, 0, 1)),
                  pl.BlockSpec((D_MODEL, COL_BLK), lambda a, b, j: (0, j))],
        out_specs=pl.BlockSpec((bn, rb, COL_BLK), lambda a, b, j: (a, b, j)),
        out_shape=jax.ShapeDtypeStruct((nb, r, IN_COLS), F32),
        scratch_shapes=[pltpu.VMEM((bn * rb, D_MODEL), BF16)],
        compiler_params=_cparams(3, VMEM_LIMIT),
        name="inproj",
    )(x, mod, mod, w_b)


def _inproj_res_kernel(x_ref, sh_ref, sc_ref, w_ref, o_ref, r1_ref, r2_ref, h_scr, res_scr):
    j = pl.program_id(2)

    @pl.when(j == 0)
    def _():
        h = x_ref[...] * (1.0 + sc_ref[...]) + sh_ref[...]
        h_scr[...] = h.reshape(h_scr.shape).astype(BF16)

    res = _dot(h_scr[...], w_ref[...])
    o_ref[0] = res
    rb = res_scr.shape[1]
    for c in range(COL_BLK // LANES):
        res_scr[c] = res[:, c * LANES:(c + 1) * LANES]
    for g, r_ref in ((1, r1_ref), (2, r2_ref)):
        dil = ATT_GROUPS[g][1]

        @pl.when((j == CB_AQ + g) | (j == CB_AK + g) | (j == CB_AV + g))
        def _(r_ref=r_ref, dil=dil):
            for r in range(dil):
                for c in range(COL_BLK // LANES):
                    r_ref[0, r, :, c * LANES:(c + 1) * LANES] = res_scr[c, pl.ds(r, rb // dil, stride=dil), :]


def _inproj_res(x, mod, w_b, rb):
    nb, r, _ = x.shape

    def res_spec(g):
        dil = ATT_GROUPS[g][1]
        return pl.BlockSpec(
            (1, dil, rb // dil, COL_BLK),
            lambda a, b, j, g=g: (a, 0, b, (j > CB_AQ + g).astype(jnp.int32) + (j > CB_AK + g).astype(jnp.int32)))

    def res_shape(g):
        dil = ATT_GROUPS[g][1]
        return jax.ShapeDtypeStruct((nb, dil, r // dil, 3 * COL_BLK), F32)

    return pl.pallas_call(
        _inproj_res_kernel,
        grid=(nb, r // rb, N_CB),
        in_specs=[pl.BlockSpec((1, rb, D_MODEL), lambda a, b, j: (a, b, 0)),
                  pl.BlockSpec((1, 1, D_MODEL), lambda a, b, j: (a, 0, 0)),
                  pl.BlockSpec((1, 1, D_MODEL), lambda a, b, j: (a, 0, 1)),
                  pl.BlockSpec((D_MODEL, COL_BLK), lambda a, b, j: (0, j))],
        out_specs=[pl.BlockSpec((1, rb, COL_BLK), lambda a, b, j: (a, b, j)), res_spec(1), res_spec(2)],
        out_shape=[jax.ShapeDtypeStruct((nb, r, IN_COLS), F32), res_shape(1), res_shape(2)],
        scratch_shapes=[pltpu.VMEM((rb, D_MODEL), BF16), pltpu.VMEM((COL_BLK // LANES, rb, LANES), F32)],
        compiler_params=_cparams(3, VMEM_LIMIT),
        name="inproj_res",
    )(x, mod, mod, w_b)


def _ret_kernel(q_ref, k_ref, v_ref, g_ref, cos_ref, sin_ref, s0_ref, z_ref, s_ref, *, bn, chunk):
    @pl.when(pl.program_id(1) == 0)
    def _():
        s_ref[...] = s0_ref[...]

    cos = cos_ref[...]
    sin = sin_ref[...]
    ri = lax.broadcasted_iota(jnp.int32, (chunk, chunk), 0)
    rj = lax.broadcasted_iota(jnp.int32, (chunk, chunk), 1)
    rel = (ri - rj).astype(F32)
    ci = lax.broadcasted_iota(jnp.int32, (chunk, 1), 0).astype(F32)
    for b in range(bn):
        for h in range(H_RET):
            sl = slice(h * DK_RET, (h + 1) * DK_RET)
            lg = LOG_GAMMA[h]
            q = q_ref[b, :, sl]
            k = k_ref[b, :, sl]
            v = v_ref[b, :, sl]
            qr = q * cos + pltpu.roll(q, DK_RET // 2, 1) * sin
            kr = (k * cos + pltpu.roll(k, DK_RET // 2, 1) * sin) * (DK_RET ** -0.5)
            intra = jnp.where(rel >= 0, jnp.exp(lg * jnp.maximum(rel, 0.0)), 0.0)
            scores = _dot_nt(qr, kr) * intra
            s = s_ref[b, h]
            o = _dot(scores, v) + _dot(qr, s) * jnp.exp(lg * (ci + 1.0))
            k_dec = jnp.exp(lg * (chunk - 1.0 - ci))
            s_ref[b, h] = s * math.exp(lg * chunk) + _dot_tn(kr * k_dec, v)
            mu = jnp.mean(o, axis=-1, keepdims=True)
            d = o - mu
            var = jnp.mean(d * d, axis=-1, keepdims=True)
            g = g_ref[b, :, sl]
            z_ref[b, :, sl] = g * jax.nn.sigmoid(g) * (d * lax.rsqrt(var + HN_EPS))


def _retention(p, cos, sin, s0, bn, chunk):
    nb, r, _ = p.shape

    def col(cb):
        return pl.BlockSpec((bn, chunk, COL_BLK), lambda a, c, cb=cb: (a, c, cb))

    st_spec = pl.BlockSpec((bn, H_RET, DK_RET, DK_RET), lambda a, c: (a, 0, 0, 0))
    return pl.pallas_call(
        functools.partial(_ret_kernel, bn=bn, chunk=chunk),
        grid=(nb // bn, r // chunk),
        in_specs=[col(CB_RQ), col(CB_RK), col(CB_RV), col(CB_RG),
                  pl.BlockSpec((chunk, DK_RET), lambda a, c: (c, 0)),
                  pl.BlockSpec((chunk, DK_RET), lambda a, c: (c, 0)),
                  st_spec],
        out_specs=[pl.BlockSpec((bn, chunk, COL_BLK), lambda a, c: (a, c, 0)), st_spec],
        out_shape=[jax.ShapeDtypeStruct((nb, r, COL_BLK), F32),
                   jax.ShapeDtypeStruct((nb, H_RET, DK_RET, DK_RET), F32)],
        compiler_params=_cparams(2, VMEM_LIMIT),
        name="retention",
    )(p, p, p, p, cos, sin, s0)


def _rotary_tables(pos0, length):
    half = DK_RET // 2
    inv = ROPE_BASE ** (-jnp.arange(half, dtype=F32) / half)
    ang = (pos0 + jnp.arange(length, dtype=jnp.int32)).astype(F32)[:, None] * inv[None, :]
    cos, sin = jnp.cos(ang), jnp.sin(ang)
    return jnp.concatenate([cos, cos], axis=-1), jnp.concatenate([-sin, sin], axis=-1)


def _t5_bucket_np(dist):
    max_exact = N_BUCKETS // 2
    d = dist.astype(np.float32)
    large = np.float32(max_exact) + (np.log(np.maximum(d, np.float32(1.0)) / np.float32(max_exact))
                                     / np.float32(math.log(MAX_DISTANCE / max_exact))
                                     * np.float32(N_BUCKETS - max_exact))
    large = np.minimum(large.astype(np.int32), N_BUCKETS - 1)
    return np.where(dist < max_exact, dist, large).astype(np.int32)


BIAS_CHUNK = 1024


def _bias_kernel(tab_ref, idx_ref, o_ref):
    idx = idx_ref[...]
    acc = jnp.zeros(o_ref.shape, F32)
    for b in range(N_BUCKETS):
        acc = jnp.where(idx == b, tab_ref[:, b:b + 1], acc)
    o_ref[...] = acc


def _bias_lookup(tab_t, idx):
    n = idx.shape[1]
    return pl.pallas_call(
        _bias_kernel,
        grid=(n // BIAS_CHUNK,),
        in_specs=[pl.BlockSpec((H_ATT, N_BUCKETS), lambda j: (0, 0)),
                  pl.BlockSpec((1, BIAS_CHUNK), lambda j: (0, j))],
        out_specs=pl.BlockSpec((H_ATT, BIAS_CHUNK), lambda j: (0, j)),
        out_shape=jax.ShapeDtypeStruct((H_ATT, n), F32),
        compiler_params=_cparams(1),
        name="bias_lookup",
    )(tab_t, idx)


def _prompt_bucket_idx(dil):
    a = np.arange(ATT_SPAN)[:, None]
    b = np.arange(2 * ATT_SPAN)[None, :]
    rel = a + ATT_SPAN - b
    return _t5_bucket_np(np.clip(rel, 0, None) * dil).reshape(1, -1)


def _sample_bucket_idx(n_buf, n_pad, t_len):
    t = np.arange(t_len)[:, None]
    i = np.arange(n_pad)[None, :]
    return _t5_bucket_np(np.clip(n_buf + t - i, 0, None)).reshape(1, -1)


def _attn_p_kernel(q_ref, kp_ref, kc_ref, vp_ref, vc_ref, bias_ref, o_ref, l_ref):
    blk = ATT_SPAN
    first = pl.program_id(2) == 0
    a = lax.broadcasted_iota(jnp.int32, (blk, 2 * blk), 0)
    b = lax.broadcasted_iota(jnp.int32, (blk, 2 * blk), 1)
    rel = a + blk - b
    k_lo = jnp.where(first, blk, 0)
    valid = (rel >= 0) & (rel <= ATT_SPAN) & (b >= k_lo)
    for h in range(H_ATT):
        sl = slice(h * DH_ATT, (h + 1) * DH_ATT)
        q = q_ref[0, 0, :, sl].astype(BF16)
        k = jnp.concatenate([kp_ref[0, 0, :, sl], kc_ref[0, 0, :, sl]], axis=0).astype(BF16)
        v = jnp.concatenate([vp_ref[0, 0, :, sl], vc_ref[0, 0, :, sl]], axis=0).astype(BF16)
        logits = _dot_nt(q, k) * (DH_ATT ** -0.5) + bias_ref[h]
        logits = jnp.where(valid, logits, NEG)
        m = jnp.max(logits, axis=-1, keepdims=True)
        p = jnp.exp(logits - m)
        l = jnp.sum(p, axis=-1, keepdims=True)
        o_ref[0, 0, :, sl] = _dot(p.astype(BF16), v) / l
        l_ref[0, 0, :, sl] = jnp.broadcast_to(m + jnp.log(l), (blk, DH_ATT))


def _attn_prompt(src, bias, cbs):
    nb, dil, n, _ = src.shape
    cb_q, cb_k, cb_v = cbs

    def cur(cb):
        return pl.BlockSpec((1, 1, ATT_SPAN, COL_BLK), lambda bb, r, i, cb=cb: (bb, r, i, cb))

    def prev(cb):
        return pl.BlockSpec((1, 1, ATT_SPAN, COL_BLK),
                            lambda bb, r, i, cb=cb: (bb, r, jnp.maximum(i - 1, 0), cb))

    out_spec = pl.BlockSpec((1, 1, ATT_SPAN, ATT_W), lambda bb, r, i: (bb, r, i, 0))
    return pl.pallas_call(
        _attn_p_kernel,
        grid=(nb, dil, n // ATT_SPAN),
        in_specs=[cur(cb_q), prev(cb_k), cur(cb_k), prev(cb_v), cur(cb_v),
                  pl.BlockSpec((H_ATT, ATT_SPAN, 2 * ATT_SPAN), lambda bb, r, i: (0, 0, 0))],
        out_specs=[out_spec, out_spec],
        out_shape=[jax.ShapeDtypeStruct((nb, dil, n, ATT_W), F32)] * 2,
        compiler_params=_cparams(3, VMEM_LIMIT),
        name="attn_prompt",
    )(src, src, src, src, src, bias)


def _attn_s_kernel(q_ref, k_ref, v_ref, c_ref, bias_ref, o_ref, l_ref, nc_ref, kall, vall, *, n_buf, dil):
    t_len = q_ref.shape[1]
    n_pad = kall.shape[0]

    @pl.when(pl.program_id(0) == 0)
    def _():
        kall[n_buf + t_len:, :] = jnp.zeros((n_pad - n_buf - t_len, ATT_W), F32)
        vall[n_buf + t_len:, :] = jnp.zeros((n_pad - n_buf - t_len, ATT_W), F32)

    kall[0:n_buf, :] = c_ref[0, :, 0:ATT_W]
    vall[0:n_buf, :] = c_ref[0, :, ATT_W:2 * ATT_W]
    kall[n_buf:n_buf + t_len, :] = k_ref[0]
    vall[n_buf:n_buf + t_len, :] = v_ref[0]
    nc_ref[0, :, 0:ATT_W] = kall[t_len:n_buf + t_len, :]
    nc_ref[0, :, ATT_W:2 * ATT_W] = vall[t_len:n_buf + t_len, :]

    t = lax.broadcasted_iota(jnp.int32, (t_len, n_pad), 0)
    i = lax.broadcasted_iota(jnp.int32, (t_len, n_pad), 1)
    d = n_buf + t - i
    valid = (d >= 0) & ((d & (dil - 1)) == 0) & (d <= ATT_SPAN * dil)
    for h in range(H_ATT):
        sl = slice(h * DH_ATT, (h + 1) * DH_ATT)
        q = q_ref[0, :, sl].astype(BF16)
        k = kall[:, sl].astype(BF16)
        v = vall[:, sl].astype(BF16)
        logits = _dot_nt(q, k) * (DH_ATT ** -0.5) + bias_ref[h]
        logits = jnp.where(valid, logits, NEG)
        m = jnp.max(logits, axis=-1, keepdims=True)
        p = jnp.exp(logits - m)
        l = jnp.sum(p, axis=-1, keepdims=True)
        o_ref[0, :, sl] = _dot(p.astype(BF16), v) / l
        l_ref[0, :, sl] = jnp.broadcast_to(m + jnp.log(l), (t_len, DH_ATT))


def _attn_sample(p, cache, bias, g, dil):
    nb, t_len, _ = p.shape
    n_buf = cache.shape[1]
    n_pad = bias.shape[2]
    cv = cache.reshape(nb, n_buf, 2 * ATT_W)

    def col(cb):
        return pl.BlockSpec((1, t_len, COL_BLK), lambda n, cb=cb: (n, 0, cb))

    small = pl.BlockSpec((1, t_len, ATT_W), lambda n: (n, 0, 0))
    big = pl.BlockSpec((1, n_buf, 2 * ATT_W), lambda n: (n, 0, 0))
    o, l, nc = pl.pallas_call(
        functools.partial(_attn_s_kernel, n_buf=n_buf, dil=dil),
        grid=(nb,),
        in_specs=[col(CB_AQ + g), col(CB_AK + g), col(CB_AV + g), big,
                  pl.BlockSpec((H_ATT, t_len, n_pad), lambda n: (0, 0, 0))],
        out_specs=[small, small, big],
        out_shape=[jax.ShapeDtypeStruct((nb, t_len, ATT_W), F32)] * 2
                  + [jax.ShapeDtypeStruct((nb, n_buf, 2 * ATT_W), F32)],
        scratch_shapes=[pltpu.VMEM((n_pad, ATT_W), F32), pltpu.VMEM((n_pad, ATT_W), F32)],
        compiler_params=_cparams(1, VMEM_LIMIT),
        name="attn_sample",
    )(p, p, p, cv, bias)
    return o, l, nc


def _merge_kernel(o0, o1, o2, l0, l1, l2, z_ref, gra, grb, gaa, gab, x_ref, gt1_ref, sh2_ref, sc2_ref,
                  wro, wao, wout, bout, lng, lnb, wq, keys, x1_ref, st_ref, il_scr):
    tm = st_ref.shape[1]

    def flat(ref):
        if len(ref.shape) == 4:
            dil = ref.shape[1]
            if dil == 1:
                return ref[0, 0]
            n_c = ATT_W // LANES
            for r in range(dil):
                for c in range(n_c):
                    il_scr[c, pl.ds(r, tm // dil, stride=dil), :] = ref[0, r, :, c * LANES:(c + 1) * LANES]
            return jnp.concatenate([il_scr[c] for c in range(n_c)], axis=-1)
        v = ref[...]
        return v.reshape(tm, v.shape[-1])

    la, lb, lc = flat(l0), flat(l1), flat(l2)
    m = jnp.maximum(jnp.maximum(la, lb), lc)
    ea, eb, ec = jnp.exp(la - m), jnp.exp(lb - m), jnp.exp(lc - m)
    att = (ea * flat(o0) + eb * flat(o1) + ec * flat(o2)) / (ea + eb + ec)
    att_y = _dot(att.astype(BF16), wao[...])
    ret_y = _dot(flat(z_ref).astype(BF16), wro[...])
    g_ret = jnp.concatenate([flat(gra), flat(grb)], axis=-1)
    g_att = jnp.concatenate([flat(gaa), flat(gab)], axis=-1)
    zz = jax.nn.sigmoid(g_ret) * ret_y + jax.nn.sigmoid(g_att) * att_y
    mix = _dot(zz.astype(BF16), wout[...]) + bout[...]
    x = x_ref[...]
    y = ALPHA * x + gt1_ref[...] * mix.reshape(x.shape)
    x1 = _layer_norm(y, lng[...], lnb[...])
    x1_ref[...] = x1
    h2 = (x1 * (1.0 + sc2_ref[...]) + sh2_ref[...]).reshape(tm, D_MODEL)
    q = _dot(h2.astype(BF16), wq[...]).astype(BF16)
    for hs in range(N_HS):
        st_ref[hs * N_KEYS:(hs + 1) * N_KEYS, :] = _dot_nt(keys[hs], q[:, hs * N_KEYS:(hs + 1) * N_KEYS])


def _merge(p, outs, lses, z, x, mod, w, bn, rb):
    nb, r, _ = x.shape
    tm = bn * rb
    n_tok = nb * r

    def rows(width, cb=0):
        return pl.BlockSpec((bn, rb, width), lambda a, b, cb=cb: (a, b, cb))

    def modspec(j):
        return pl.BlockSpec((bn, 1, D_MODEL), lambda a, b, j=j: (a, 0, j))

    def full(arr):
        nd = arr.ndim
        return pl.BlockSpec(arr.shape, lambda a, b, nd=nd: (0,) * nd)

    weights = [w["w_ret_o"], w["w_att_o"], w["w_out"], w["b_out"], w["ln1_g"], w["ln1_b"], w["peer_wq"],
               w["peer_keys"]]
    nrb = r // rb

    def group_spec(arr):
        if arr.ndim == 3:
            return rows(ATT_W)
        dil = arr.shape[1]
        return pl.BlockSpec((1, dil, tm // dil, ATT_W), lambda a, b: (a, 0, b, 0))

    x1, st = pl.pallas_call(
        _merge_kernel,
        grid=(nb // bn, nrb),
        in_specs=[group_spec(a) for a in (*outs, *lses)] + [rows(ATT_W)]
                 + [rows(COL_BLK, CB_GRET), rows(COL_BLK, CB_GRET + 1),
                    rows(COL_BLK, CB_GATT), rows(COL_BLK, CB_GATT + 1),
                    rows(D_MODEL), modspec(2), modspec(3), modspec(4)]
                 + [full(a) for a in weights],
        out_specs=[rows(D_MODEL),
                   pl.BlockSpec((N_HS * N_KEYS, tm), lambda a, b: (0, a * nrb + b))],
        out_shape=[jax.ShapeDtypeStruct((nb, r, D_MODEL), F32),
                   jax.ShapeDtypeStruct((N_HS * N_KEYS, n_tok), F32)],
        scratch_shapes=[pltpu.VMEM((ATT_W // LANES, tm, LANES), F32)],
        compiler_params=_cparams(2, VMEM_LIMIT),
        name="merge",
    )(*outs, *lses, z, p, p, p, p, x, mod, mod, mod, *weights)
    return x1, st


TOPK_TOK = 128
SUB = 8
CAND_ROWS = [TOPK // (i + 1) for i in range(TOPK)]
N_CAND = sum(CAND_ROWS)
N_CAND_PAD = -(-N_CAND // SUB) * SUB
CAND_PAD_IDX = TOPK * TOPK


def _cand_flat_index():
    flat = [i * TOPK + j for i in range(TOPK) for j in range(CAND_ROWS[i])]
    flat += [CAND_PAD_IDX] * (N_CAND_PAD - N_CAND)
    return np.broadcast_to(np.asarray(flat, np.int32)[:, None], (N_CAND_PAD, TOPK_TOK)).copy()


def _argmax_cols(xs, idxs):
    while len(xs) > 1:
        nx, ni = [], []
        for k in range(0, len(xs) - 1, 2):
            nx.append(jnp.maximum(xs[k], xs[k + 1]))
            ni.append(jnp.where(xs[k] >= xs[k + 1], idxs[k], idxs[k + 1]))
        if len(xs) % 2:
            nx.append(xs[-1])
            ni.append(idxs[-1])
        xs, idxs = nx, ni
    v, i = xs[0], idxs[0]
    for shift in (4, 2, 1):
        v2 = pltpu.roll(v, shift, 0)
        i2 = pltpu.roll(i, shift, 0)
        take = (v > v2) | ((v == v2) & (i < i2))
        v = jnp.where(take, v, v2)
        i = jnp.where(take, i, i2)
    return v, i


def _select16(xs, idxs, val_ref, idx_ref):
    xs = list(xs)
    for r in range(TOPK):
        v, i = _argmax_cols(xs, idxs)
        val_ref[pl.ds(r, 1), :] = v[0:1]
        idx_ref[pl.ds(r, 1), :] = i[0:1]
        xs = [jnp.where(ik == i, -jnp.inf, xk) for xk, ik in zip(xs, idxs)]


def _topk_kernel(st_ref, cidx_ref, e_ref, g_ref, sv, si, cand, cv, ci):
    tok = st_ref.shape[1]
    sub_iota = lax.broadcasted_iota(jnp.int32, (SUB, tok), 0)
    key_idx = [sub_iota + SUB * k for k in range(N_KEYS // SUB)]
    cand_idx = [cidx_ref[SUB * k:SUB * (k + 1), :] for k in range(N_CAND_PAD // SUB)]

    def head(h, carry):
        for half in range(2):
            base = pl.multiple_of((2 * h + half) * N_KEYS, N_KEYS)
            xs = [st_ref[pl.ds(base + SUB * k, SUB), :] for k in range(N_KEYS // SUB)]
            _select16(xs, key_idx, sv.at[half], si.at[half])
        a = sv[0]
        b = sv[1]
        cand[N_CAND_PAD - SUB:, :] = jnp.full((SUB, tok), -jnp.inf, F32)
        off = 0
        for i in range(TOPK):
            cand[off:off + CAND_ROWS[i], :] = a[i:i + 1, :] + b[0:CAND_ROWS[i], :]
            off += CAND_ROWS[i]
        _select16([cand[SUB * k:SUB * (k + 1), :] for k in range(N_CAND_PAD // SUB)], cand_idx, cv, ci)
        c_val = cv[...]
        c_idx = ci[...]
        ia = lax.shift_right_logical(c_idx, 4)
        ib = c_idx & (TOPK - 1)
        sa = si[0]
        sb = si[1]
        i0 = jnp.zeros_like(c_idx)
        i1 = jnp.zeros_like(c_idx)
        for i in range(TOPK):
            i0 = jnp.where(ia == i, sa[i:i + 1, :], i0)
            i1 = jnp.where(ib == i, sb[i:i + 1, :], i1)
        ex = jnp.exp(c_val - c_val[0:1, :])
        rows = pl.ds(pl.multiple_of(h * TOPK, TOPK), TOPK)
        e_ref[rows, :] = i0 * N_KEYS + i1
        g_ref[rows, :] = ex / jnp.sum(ex, axis=0, keepdims=True)
        return carry

    lax.fori_loop(0, H_PEER, head, 0)


def _topk(st):
    n_tok = st.shape[1]
    tt = TOPK_TOK
    spec = pl.BlockSpec((N_SEL, tt), lambda i: (0, i))
    return pl.pallas_call(
        _topk_kernel,
        grid=(n_tok // tt,),
        in_specs=[pl.BlockSpec((N_HS * N_KEYS, tt), lambda i: (0, i)),
                  pl.BlockSpec((N_CAND_PAD, tt), lambda i: (0, 0))],
        out_specs=[spec, spec],
        out_shape=[jax.ShapeDtypeStruct((N_SEL, n_tok), jnp.int32),
                   jax.ShapeDtypeStruct((N_SEL, n_tok), F32)],
        scratch_shapes=[pltpu.VMEM((2, TOPK, tt), F32), pltpu.VMEM((2, TOPK, tt), jnp.int32),
                        pltpu.VMEM((N_CAND_PAD, tt), F32),
                        pltpu.VMEM((TOPK, tt), F32), pltpu.VMEM((TOPK, tt), jnp.int32)],
        compiler_params=_cparams(1, VMEM_LIMIT),
        name="topk",
    )(st, jnp.asarray(_cand_flat_index()))


G_LANES = 128
LANES = 128
PV_TOK = 128
HALF = D_MODEL // 2
ROW_SUB = HALF // LANES
TILE_STRIDE = 136
HI_MASK = -65536


def _pack_halves(t):
    tb = lax.bitcast_convert_type(t.astype(BF16), jnp.uint16).astype(jnp.uint32)
    word = (tb[:, :HALF] << 16) | tb[:, HALF:]
    return lax.bitcast_convert_type(word, jnp.int32).reshape(-1, LANES)


def _load_table(tab_hbm, tab, sem):
    @pl.when(pl.program_id(0) == 0)
    def _():
        cp = pltpu.make_async_copy(tab_hbm, tab, sem.at[0])
        cp.start()
        cp.wait()


def _gather_rows(e_ref, first, tab, tile):
    for r in range(N_SEL):
        off = pl.multiple_of(e_ref[first + r], ROW_SUB)
        tile[pl.ds(r, ROW_SUB, stride=TILE_STRIDE), :] = tab[pl.ds(off, ROW_SUB), :]


def _tile_halves(tile, c):
    words = tile[c * TILE_STRIDE:c * TILE_STRIDE + N_SEL, :]
    hi = lax.bitcast_convert_type(words & HI_MASK, F32)
    lo = lax.bitcast_convert_type(words << 16, F32)
    return hi, lo


TOK_GRP = 8


def _token_loop(e_ref, tab, tiles, compute, finish_group, init):
    _gather_rows(e_ref, 0, tab, tiles[0])


    def body(k, carry):
        grp = k * (TOK_GRP * N_SEL)
        e_grp = e_ref.at[pl.ds(grp, TOK_GRP * N_SEL)]
        outs = []
        for s in range(TOK_GRP):
            if s + 1 < TOK_GRP:
                _gather_rows(e_grp, (s + 1) * N_SEL, tab, tiles[(s + 1) % 2])
            else:
                _gather_rows(e_ref, jnp.minimum(grp + TOK_GRP * N_SEL, (PV_TOK - 1) * N_SEL), tab, tiles[0])
            carry, out = compute(k * TOK_GRP + s, tiles[s % 2], carry)
            outs.append(out)
        finish_group(k, outs)
        return carry

    return lax.fori_loop(0, PV_TOK // TOK_GRP, body, init)


def _pick_token(j):
    tok_row = lax.broadcasted_iota(jnp.int32, (G_LANES, N_SEL), 0)
    return jnp.where(tok_row == j, 1.0, 0.0).astype(BF16)


def _peer_u_kernel(e_ref, gt_ref, x1_ref, sh2_ref, sc2_ref, tab_hbm, w_ref, tab, sem, tile0, tile1, h2_scr):
    _load_table(tab_hbm, tab, sem)
    h2 = x1_ref[...] * (1.0 + sc2_ref[...]) + sh2_ref[...]
    h2_scr[...] = h2.reshape(PV_TOK, D_MODEL)
    ones = jnp.ones((N_SEL, N_SEL), BF16)
    lane = lax.broadcasted_iota(jnp.int32, (N_SEL, G_LANES), 1)

    def compute(j, tile, wt):
        x = h2_scr[pl.ds(j, 1), :]
        part = None
        for c in range(ROW_SUB):
            hi, lo = _tile_halves(tile, c)
            t = hi * x[:, c * LANES:(c + 1) * LANES] + lo * x[:, HALF + c * LANES:HALF + (c + 1) * LANES]
            part = t if part is None else part + t
        a = _dot_exact(part, ones)
        act = 0.5 * a * (1.0 + lax.erf(a * (2.0 ** -0.5)))
        w = _dot_exact(gt_ref[...], _pick_token(j)) * act
        return jnp.where(lane == j, w, wt), None

    w_ref[...] = _token_loop(e_ref, tab, (tile0, tile1), compute, lambda k, outs: None,
                             jnp.zeros((N_SEL, G_LANES), F32))


def _peer_v_kernel(e_ref, w_ref, x1_ref, gt2_ref, lng, lnb, tab_hbm, o_ref, tab, sem, tile0, tile1, y_scr):
    _load_table(tab_hbm, tab, sem)

    def compute(j, tile, carry):
        w = _dot_exact(w_ref[...], _pick_token(j))
        his, los = [], []
        for c in range(ROW_SUB):
            hi, lo = _tile_halves(tile, c)
            his.append(jnp.sum(hi * w, axis=0, keepdims=True))
            los.append(jnp.sum(lo * w, axis=0, keepdims=True))
        return carry, jnp.concatenate(his + los, axis=-1)

    def finish_group(k, rows):
        y_scr[pl.ds(pl.multiple_of(k * TOK_GRP, TOK_GRP), TOK_GRP), :] = jnp.concatenate(rows, axis=0)

    _token_loop(e_ref, tab, (tile0, tile1), compute, finish_group, 0)
    x1 = x1_ref[...]
    y = ALPHA * x1 + gt2_ref[...] * y_scr[...].reshape(x1.shape)
    o_ref[...] = _layer_norm(y, lng[...], lnb[...])


def _peer(e_rows, g_t, x1, mod, tab_u, tab_v, ln_g, ln_b):
    nb, r, _ = x1.shape
    n_tok = nb * r
    steps = n_tok // PV_TOK
    bn, rb = (1, PV_TOK) if r >= PV_TOK else (PV_TOK // r, r)
    nrb = r // rb

    def modspec(j):
        return pl.BlockSpec((bn, 1, D_MODEL), lambda i, j=j: (i // nrb, 0, j))

    e_spec = pl.BlockSpec((PV_TOK * N_SEL,), lambda i: (i,), memory_space=pltpu.SMEM)
    gate_spec = pl.BlockSpec((N_SEL, G_LANES), lambda i: (0, i))
    x_spec = pl.BlockSpec((bn, rb, D_MODEL), lambda i: (i // nrb, i % nrb, 0))
    vec = pl.BlockSpec((1, D_MODEL), lambda i: (0, 0))
    table = pl.BlockSpec(memory_space=pl.ANY)
    tile = pltpu.VMEM((ROW_SUB * TILE_STRIDE, LANES), jnp.int32)
    scratch = [pltpu.VMEM(tab_u.shape, jnp.int32), pltpu.SemaphoreType.DMA((1,)), tile, tile,
               pltpu.VMEM((PV_TOK, D_MODEL), F32)]
    w_t = pl.pallas_call(
        _peer_u_kernel,
        grid=(steps,),
        in_specs=[e_spec, gate_spec, x_spec, modspec(3), modspec(4), table],
        out_specs=gate_spec,
        out_shape=jax.ShapeDtypeStruct((N_SEL, n_tok), F32),
        scratch_shapes=scratch,
        compiler_params=_cparams(1, VMEM_LIMIT),
        name="peer_u",
    )(e_rows, g_t, x1, mod, mod, tab_u)
    return pl.pallas_call(
        _peer_v_kernel,
        grid=(steps,),
        in_specs=[e_spec, gate_spec, x_spec, modspec(5), vec, vec, table],
        out_specs=x_spec,
        out_shape=jax.ShapeDtypeStruct((nb, r, D_MODEL), F32),
        scratch_shapes=scratch,
        compiler_params=_cparams(1, VMEM_LIMIT),
        name="peer_v",
    )(e_rows, w_t, x1, mod, ln_g, ln_b, tab_v)


def _branch(x, mod, pos0, s0, caches, biases, w, uv, *, prompt):
    nb, r, _ = x.shape
    if prompt:
        bn, rb_proj, rb_merge, chunk, bn_ret = 1, 1024, 256, RET_CHUNK, 1
    else:
        bn, rb_proj, rb_merge, chunk, bn_ret = 32, r, r, math.gcd(r, RET_CHUNK), 8
    if prompt:
        p, res1, res2 = _inproj_res(x, mod, w["w_in"], rb_proj)
        att_src = ((p.reshape(nb, 1, r, IN_COLS), (CB_AQ, CB_AK, CB_AV)), (res1, (0, 1, 2)), (res2, (0, 1, 2)))
    else:
        p = _inproj(x, mod, w["w_in"], bn, rb_proj)
    cos, sin = _rotary_tables(pos0, r)
    z, s_new = _retention(p, cos, sin, s0, bn_ret, chunk)
    outs, lses, new_bufs = [], [], []
    for g, (win, dil) in enumerate(ATT_GROUPS):
        if prompt:
            o, l = _attn_prompt(att_src[g][0], biases[g], att_src[g][1])
            nw = min(win, r)
            kg = p[:, r - nw:, (CB_AK + g) * COL_BLK:(CB_AK + g + 1) * COL_BLK]
            vg = p[:, r - nw:, (CB_AV + g) * COL_BLK:(CB_AV + g + 1) * COL_BLK]
            nbuf = jnp.stack([kg, vg], axis=2).reshape(nb, nw, 2, H_ATT, DH_ATT)
        else:
            o, l, nc = _attn_sample(p, caches[g], biases[g], g, dil)
            nbuf = nc.reshape(nb, nc.shape[1], 2, H_ATT, DH_ATT)
        outs.append(o)
        lses.append(l)
        new_bufs.append(nbuf)
    x1, st = _merge(p, outs, lses, z, x, mod, w, bn, rb_merge)
    e_t, g_t = _topk(st)
    y = _peer(e_t.T.reshape(-1) * ROW_SUB, g_t, x1, mod, uv[0], uv[1], w["ln2_g"], w["ln2_b"])
    return y, s_new, new_bufs


def kernel(x_prompt, x_sample, state_ret, cache_att_w128, cache_att_w512, cache_att_w2048, c_prompt, c_sample,
           w_ada, b_ada, w_in, w_ret_o, w_att_o, w_out, b_out, ln1_g, ln1_b, peer_wq, peer_keys, peer_u, peer_v,
           ln2_g, ln2_b, rel_bias):
    nbp, seq, _ = x_prompt.shape
    nbs, dec_seq, _ = x_sample.shape
    w = {
        "w_in": w_in[0].astype(BF16),
        "w_ret_o": w_ret_o[0].astype(BF16),
        "w_att_o": w_att_o[0].astype(BF16),
        "w_out": w_out[0].astype(BF16),
        "b_out": b_out[0].reshape(1, -1),
        "ln1_g": ln1_g[0].reshape(1, -1),
        "ln1_b": ln1_b[0].reshape(1, -1),
        "peer_wq": peer_wq[0].astype(BF16),
        "peer_keys": peer_keys[0].reshape(N_HS, N_KEYS, N_KEYS).astype(BF16),
        "ln2_g": ln2_g[0].reshape(1, -1),
        "ln2_b": ln2_b[0].reshape(1, -1),
    }
    uv = (_pack_halves(peer_u[0]), _pack_halves(peer_v[0]))
    mod = _ada(jnp.concatenate([c_prompt, c_sample], axis=0), w_ada[0], b_ada[0])
    mod_p = mod[:nbp].reshape(nbp, 1, -1)
    mod_s = mod[nbp:].reshape(nbs, 1, -1)

    caches = (cache_att_w128[0], cache_att_w512[0], cache_att_w2048[0])
    bias_p, bias_s = [], []
    for g, (win, dil) in enumerate(ATT_GROUPS):
        n_buf = caches[g].shape[1]
        n_pad = n_buf + ATT_SPAN
        idx = np.concatenate([_prompt_bucket_idx(dil), _sample_bucket_idx(n_buf, n_pad, dec_seq)], axis=1)
        tab = _bias_lookup(rel_bias[:, g * H_ATT:(g + 1) * H_ATT].T, jnp.asarray(idx))
        n_p = ATT_SPAN * 2 * ATT_SPAN
        bias_p.append(tab[:, :n_p].reshape(H_ATT, ATT_SPAN, 2 * ATT_SPAN))
        bias_s.append(tab[:, n_p:].reshape(H_ATT, dec_seq, n_pad))

    s0_p = jnp.zeros((nbp, H_RET, DK_RET, DK_RET), F32)
    ys, rs, bs = _branch(x_sample, mod_s, PAST_LEN, state_ret[0], caches, bias_s, w, uv, prompt=False)
    yp, rp, bp = _branch(x_prompt, mod_p, 0, s0_p, None, bias_p, w, uv, prompt=True)
    return (yp, ys, rp[None], bp[0][None], bp[1][None], bp[2][None],
            rs[None], bs[0][None], bs[1][None], bs[2][None])
```

```python
import functools
import math

import numpy as np
import jax
import jax.numpy as jnp
from jax import lax
from jax.experimental import pallas as pl
from jax.experimental.pallas import tpu as pltpu

F32 = jnp.float32
BF16 = jnp.bfloat16

D_MODEL = 1024
PAST_LEN = 8192
H_RET = 4
DK_RET = 128
RET_CHUNK = 128
ROPE_BASE = 10000.0
ATT_GROUPS = ((128, 1), (512, 4), (2048, 16))
N_GROUPS = 3
H_ATT = 8
DH_ATT = 64
ATT_SPAN = 128
ATT_W = H_ATT * DH_ATT
NEG = -1e30
N_BUCKETS = 32
MAX_DISTANCE = 2048
N_KEYS = 128
H_PEER = 8
TOPK = 16
N_HS = 2 * H_PEER
N_SEL = H_PEER * TOPK
ALPHA = 2.0 ** 0.25
LN_EPS = 1e-5
HN_EPS = 1e-6
IN_COLS = 8704
COL_BLK = 512
CB_RQ, CB_RK, CB_RV, CB_RG = 0, 1, 2, 3
CB_AQ, CB_AK, CB_AV = 4, 7, 10
CB_GRET, CB_GATT = 13, 15
N_CB = IN_COLS // COL_BLK
LOG_GAMMA = [math.log1p(-(2.0 ** (-5.0 - h))) for h in range(H_RET)]
VMEM_LIMIT = 56 * 1024 * 1024


def _cparams(n_axes, vmem=None):
    return pltpu.CompilerParams(dimension_semantics=("arbitrary",) * n_axes,
                                vmem_limit_bytes=vmem)


def _dot(a, b):
    return jnp.dot(a, b, preferred_element_type=F32)


def _dot_nt(a, b):
    return lax.dot_general(a, b, (((1,), (1,)), ((), ())), preferred_element_type=F32)


def _dot_tn(a, b):
    return lax.dot_general(a, b, (((0,), (0,)), ((), ())), preferred_element_type=F32)


def _dot_exact(x, sel):
    hi = x.astype(BF16)
    r1 = x - hi.astype(F32)
    mid = r1.astype(BF16)
    lo = (r1 - mid.astype(F32)).astype(BF16)
    return _dot(hi, sel) + _dot(mid, sel) + _dot(lo, sel)


def _layer_norm(y, g, b):
    mu = jnp.mean(y, axis=-1, keepdims=True)
    d = y - mu
    var = jnp.mean(d * d, axis=-1, keepdims=True)
    return d * lax.rsqrt(var + LN_EPS) * g + b


def _ada_kernel(c_ref, w_ref, b_ref, o_ref):
    c = c_ref[...]
    s = c * jax.nn.sigmoid(c)
    o_ref[...] = _dot(s.astype(BF16), w_ref[...].astype(BF16)) + b_ref[...]


def _ada(c, w, b):
    n = c.shape[0]
    return pl.pallas_call(
        _ada_kernel,
        grid=(6,),
        in_specs=[pl.BlockSpec((n, D_MODEL), lambda j: (0, 0)),
                  pl.BlockSpec((D_MODEL, D_MODEL), lambda j: (0, j)),
                  pl.BlockSpec((1, D_MODEL), lambda j: (0, j))],
        out_specs=pl.BlockSpec((n, D_MODEL), lambda j: (0, j)),
        out_shape=jax.ShapeDtypeStruct((n, 6 * D_MODEL), F32),
        compiler_params=_cparams(1, VMEM_LIMIT),
        name="ada",
    )(c, w, b.reshape(1, -1))


def _inproj_kernel(x_ref, sh_ref, sc_ref, w_ref, o_ref, h_scr):
    @pl.when(pl.program_id(2) == 0)
    def _():
        h = x_ref[...] * (1.0 + sc_ref[...]) + sh_ref[...]
        h_scr[...] = h.reshape(h_scr.shape).astype(BF16)

    o_ref[...] = _dot(h_scr[...], w_ref[...]).reshape(o_ref.shape)


def _inproj(x, mod, w_b, bn, rb):
    nb, r, _ = x.shape
    return pl.pallas_call(
        _inproj_kernel,
        grid=(nb // bn, r // rb, N_CB),
        in_specs=[pl.BlockSpec((bn, rb, D_MODEL), lambda a, b, j: (a, b, 0)),
                  pl.BlockSpec((bn, 1, D_MODEL), lambda a, b, j: (a, 0, 0)),
                  pl.BlockSpec((bn, 1, D_MODEL), lambda a, b, j: (a, 0, 1)),
                  pl.BlockSpec((D_MODEL, COL_BLK), lambda a, b, j: (0, j))],
        out_specs=pl.BlockSpec((bn, rb, COL_BLK), lambda a, b, j: (a, b, j)),
        out_shape=jax.ShapeDtypeStruct((nb, r, IN_COLS), F32),
        scratch_shapes=[pltpu.VMEM((bn * rb, D_MODEL), BF16)],
        compiler_params=_cparams(3, VMEM_LIMIT),
        name="inproj",
    )(x, mod, mod, w_b)


def _inproj_res_kernel(x_ref, sh_ref, sc_ref, w_ref, o_ref, r1_ref, r2_ref, h_scr, res_scr):
    j = pl.program_id(2)

    @pl.when(j == 0)
    def _():
        h = x_ref[...] * (1.0 + sc_ref[...]) + sh_ref[...]
        h_scr[...] = h.reshape(h_scr.shape).astype(BF16)

    res = _dot(h_scr[...], w_ref[...])
    o_ref[0] = res
    rb = res_scr.shape[1]
    for c in range(COL_BLK // LANES):
        res_scr[c] = res[:, c * LANES:(c + 1) * LANES]
    for g, r_ref in ((1, r1_ref), (2, r2_ref)):
        dil = ATT_GROUPS[g][1]

        @pl.when((j == CB_AQ + g) | (j == CB_AK + g) | (j == CB_AV + g))
        def _(r_ref=r_ref, dil=dil):
            for r in range(dil):
                for c in range(COL_BLK // LANES):
                    r_ref[0, r, :, c * LANES:(c + 1) * LANES] = res_scr[c, pl.ds(r, rb // dil, stride=dil), :]


def _inproj_res(x, mod, w_b, rb):
    nb, r, _ = x.shape

    def res_spec(g):
        dil = ATT_GROUPS[g][1]
        return pl.BlockSpec(
            (1, dil, rb // dil, COL_BLK),
            lambda a, b, j, g=g: (a, 0, b, (j > CB_AQ + g).astype(jnp.int32) + (j > CB_AK + g).astype(jnp.int32)))

    def res_shape(g):
        dil = ATT_GROUPS[g][1]
        return jax.ShapeDtypeStruct((nb, dil, r // dil, 3 * COL_BLK), F32)

    return pl.pallas_call(
        _inproj_res_kernel,
        grid=(nb, r // rb, N_CB),
        in_specs=[pl.BlockSpec((1, rb, D_MODEL), lambda a, b, j: (a, b, 0)),
                  pl.BlockSpec((1, 1, D_MODEL), lambda a, b, j: (a, 0, 0)),
                  pl.BlockSpec((1, 1, D_MODEL), lambda a, b, j: (a, 0, 1)),
                  pl.BlockSpec((D_MODEL, COL_BLK), lambda a, b, j: (0, j))],
        out_specs=[pl.BlockSpec((1, rb, COL_BLK), lambda a, b, j: (a, b, j)), res_spec(1), res_spec(2)],
        out_shape=[jax.ShapeDtypeStruct((nb, r, IN_COLS), F32), res_shape(1), res_shape(2)],
        scratch_shapes=[pltpu.VMEM((rb, D_MODEL), BF16), pltpu.VMEM((COL_BLK // LANES, rb, LANES), F32)],
        compiler_params=_cparams(3, VMEM_LIMIT),
        name="inproj_res",
    )(x, mod, mod, w_b)


def _ret_kernel(q_ref, k_ref, v_ref, g_ref, cos_ref, sin_ref, s0_ref, z_ref, s_ref, *, bn, chunk):
    @pl.when(pl.program_id(1) == 0)
    def _():
        s_ref[...] = s0_ref[...]

    cos = cos_ref[...]
    sin = sin_ref[...]
    ri = lax.broadcasted_iota(jnp.int32, (chunk, chunk), 0)
    rj = lax.broadcasted_iota(jnp.int32, (chunk, chunk), 1)
    rel = (ri - rj).astype(F32)
    ci = lax.broadcasted_iota(jnp.int32, (chunk, 1), 0).astype(F32)
    for b in range(bn):
        for h in range(H_RET):
            sl = slice(h * DK_RET, (h + 1) * DK_RET)
            lg = LOG_GAMMA[h]
            q = q_ref[b, :, sl]
            k = k_ref[b, :, sl]
            v = v_ref[b, :, sl]
            qr = q * cos + pltpu.roll(q, DK_RET // 2, 1) * sin
            kr = (k * cos + pltpu.roll(k, DK_RET // 2, 1) * sin) * (DK_RET ** -0.5)
            intra = jnp.where(rel >= 0, jnp.exp(lg * jnp.maximum(rel, 0.0)), 0.0)
            scores = _dot_nt(qr, kr) * intra
            s = s_ref[b, h]
            o = _dot(scores, v) + _dot(qr, s) * jnp.exp(lg * (ci + 1.0))
            k_dec = jnp.exp(lg * (chunk - 1.0 - ci))
            s_ref[b, h] = s * math.exp(lg * chunk) + _dot_tn(kr * k_dec, v)
            mu = jnp.mean(o, axis=-1, keepdims=True)
            d = o - mu
            var = jnp.mean(d * d, axis=-1, keepdims=True)
            g = g_ref[b, :, sl]
            z_ref[b, :, sl] = g * jax.nn.sigmoid(g) * (d * lax.rsqrt(var + HN_EPS))


def _retention(p, cos, sin, s0, bn, chunk):
    nb, r, _ = p.shape

    def col(cb):
        return pl.BlockSpec((bn, chunk, COL_BLK), lambda a, c, cb=cb: (a, c, cb))

    st_spec = pl.BlockSpec((bn, H_RET, DK_RET, DK_RET), lambda a, c: (a, 0, 0, 0))
    return pl.pallas_call(
        functools.partial(_ret_kernel, bn=bn, chunk=chunk),
        grid=(nb // bn, r // chunk),
        in_specs=[col(CB_RQ), col(CB_RK), col(CB_RV), col(CB_RG),
                  pl.BlockSpec((chunk, DK_RET), lambda a, c: (c, 0)),
                  pl.BlockSpec((chunk, DK_RET), lambda a, c: (c, 0)),
                  st_spec],
        out_specs=[pl.BlockSpec((bn, chunk, COL_BLK), lambda a, c: (a, c, 0)), st_spec],
        out_shape=[jax.ShapeDtypeStruct((nb, r, COL_BLK), F32),
                   jax.ShapeDtypeStruct((nb, H_RET, DK_RET, DK_RET), F32)],
        compiler_params=_cparams(2, VMEM_LIMIT),
        name="retention",
    )(p, p, p, p, cos, sin, s0)


def _rotary_tables(pos0, length):
    half = DK_RET // 2
    inv = ROPE_BASE ** (-jnp.arange(half, dtype=F32) / half)
    ang = (pos0 + jnp.arange(length, dtype=jnp.int32)).astype(F32)[:, None] * inv[None, :]
    cos, sin = jnp.cos(ang), jnp.sin(ang)
    return jnp.concatenate([cos, cos], axis=-1), jnp.concatenate([-sin, sin], axis=-1)


def _t5_bucket_np(dist):
    max_exact = N_BUCKETS // 2
    d = dist.astype(np.float32)
    large = np.float32(max_exact) + (np.log(np.maximum(d, np.float32(1.0)) / np.float32(max_exact))
                                     / np.float32(math.log(MAX_DISTANCE / max_exact))
                                     * np.float32(N_BUCKETS - max_exact))
    large = np.minimum(large.astype(np.int32), N_BUCKETS - 1)
    return np.where(dist < max_exact, dist, large).astype(np.int32)


BIAS_CHUNK = 1024


def _bias_kernel(tab_ref, idx_ref, o_ref):
    idx = idx_ref[...]
    acc = jnp.zeros(o_ref.shape, F32)
    for b in range(N_BUCKETS):
        acc = jnp.where(idx == b, tab_ref[:, b:b + 1], acc)
    o_ref[...] = acc


def _bias_lookup(tab_t, idx):
    n = idx.shape[1]
    return pl.pallas_call(
        _bias_kernel,
        grid=(n // BIAS_CHUNK,),
        in_specs=[pl.BlockSpec((H_ATT, N_BUCKETS), lambda j: (0, 0)),
                  pl.BlockSpec((1, BIAS_CHUNK), lambda j: (0, j))],
        out_specs=pl.BlockSpec((H_ATT, BIAS_CHUNK), lambda j: (0, j)),
        out_shape=jax.ShapeDtypeStruct((H_ATT, n), F32),
        compiler_params=_cparams(1),
        name="bias_lookup",
    )(tab_t, idx)


def _prompt_bucket_idx(dil):
    a = np.arange(ATT_SPAN)[:, None]
    b = np.arange(2 * ATT_SPAN)[None, :]
    rel = a + ATT_SPAN - b
    return _t5_bucket_np(np.clip(rel, 0, None) * dil).reshape(1, -1)


def _sample_bucket_idx(n_buf, n_pad, t_len):
    t = np.arange(t_len)[:, None]
    i = np.arange(n_pad)[None, :]
    return _t5_bucket_np(np.clip(n_buf + t - i, 0, None)).reshape(1, -1)


def _attn_p_kernel(q_ref, kp_ref, kc_ref, vp_ref, vc_ref, bias_ref, o_ref, l_ref):
    blk = ATT_SPAN
    first = pl.program_id(2) == 0
    a = lax.broadcasted_iota(jnp.int32, (blk, 2 * blk), 0)
    b = lax.broadcasted_iota(jnp.int32, (blk, 2 * blk), 1)
    rel = a + blk - b
    k_lo = jnp.where(first, blk, 0)
    valid = (rel >= 0) & (rel <= ATT_SPAN) & (b >= k_lo)
    for h in range(H_ATT):
        sl = slice(h * DH_ATT, (h + 1) * DH_ATT)
        q = q_ref[0, 0, :, sl].astype(BF16)
        k = jnp.concatenate([kp_ref[0, 0, :, sl], kc_ref[0, 0, :, sl]], axis=0).astype(BF16)
        v = jnp.concatenate([vp_ref[0, 0, :, sl], vc_ref[0, 0, :, sl]], axis=0).astype(BF16)
        logits = _dot_nt(q, k) * (DH_ATT ** -0.5) + bias_ref[h]
        logits = jnp.where(valid, logits, NEG)
        m = jnp.max(logits, axis=-1, keepdims=True)
        p = jnp.exp(logits - m)
        l = jnp.sum(p, axis=-1, keepdims=True)
        o_ref[0, 0, :, sl] = _dot(p.astype(BF16), v) / l
        l_ref[0, 0, :, sl] = jnp.broadcast_to(m + jnp.log(l), (blk, DH_ATT))


def _attn_prompt(src, bias, cbs):
    nb, dil, n, _ = src.shape
    cb_q, cb_k, cb_v = cbs

    def cur(cb):
        return pl.BlockSpec((1, 1, ATT_SPAN, COL_BLK), lambda bb, r, i, cb=cb: (bb, r, i, cb))

    def prev(cb):
        return pl.BlockSpec((1, 1, ATT_SPAN, COL_BLK),
                            lambda bb, r, i, cb=cb: (bb, r, jnp.maximum(i - 1, 0), cb))

    out_spec = pl.BlockSpec((1, 1, ATT_SPAN, ATT_W), lambda bb, r, i: (bb, r, i, 0))
    return pl.pallas_call(
        _attn_p_kernel,
        grid=(nb, dil, n // ATT_SPAN),
        in_specs=[cur(cb_q), prev(cb_k), cur(cb_k), prev(cb_v), cur(cb_v),
                  pl.BlockSpec((H_ATT, ATT_SPAN, 2 * ATT_SPAN), lambda bb, r, i: (0, 0, 0))],
        out_specs=[out_spec, out_spec],
        out_shape=[jax.ShapeDtypeStruct((nb, dil, n, ATT_W), F32)] * 2,
        compiler_params=_cparams(3, VMEM_LIMIT),
        name="attn_prompt",
    )(src, src, src, src, src, bias)


def _attn_s_kernel(q_ref, k_ref, v_ref, c_ref, bias_ref, o_ref, l_ref, nc_ref, kall, vall, *, n_buf, dil):
    t_len = q_ref.shape[1]
    n_pad = kall.shape[0]

    @pl.when(pl.program_id(0) == 0)
    def _():
        kall[n_buf + t_len:, :] = jnp.zeros((n_pad - n_buf - t_len, ATT_W), F32)
        vall[n_buf + t_len:, :] = jnp.zeros((n_pad - n_buf - t_len, ATT_W), F32)

    kall[0:n_buf, :] = c_ref[0, :, 0:ATT_W]
    vall[0:n_buf, :] = c_ref[0, :, ATT_W:2 * ATT_W]
    kall[n_buf:n_buf + t_len, :] = k_ref[0]
    vall[n_buf:n_buf + t_len, :] = v_ref[0]
    nc_ref[0, :, 0:ATT_W] = kall[t_len:n_buf + t_len, :]
    nc_ref[0, :, ATT_W:2 * ATT_W] = vall[t_len:n_buf + t_len, :]

    t = lax.broadcasted_iota(jnp.int32, (t_len, n_pad), 0)
    i = lax.broadcasted_iota(jnp.int32, (t_len, n_pad), 1)
    d = n_buf + t - i
    valid = (d >= 0) & ((d & (dil - 1)) == 0) & (d <= ATT_SPAN * dil)
    for h in range(H_ATT):
        sl = slice(h * DH_ATT, (h + 1) * DH_ATT)
        q = q_ref[0, :, sl].astype(BF16)
        k = kall[:, sl].astype(BF16)
        v = vall[:, sl].astype(BF16)
        logits = _dot_nt(q, k) * (DH_ATT ** -0.5) + bias_ref[h]
        logits = jnp.where(valid, logits, NEG)
        m = jnp.max(logits, axis=-1, keepdims=True)
        p = jnp.exp(logits - m)
        l = jnp.sum(p, axis=-1, keepdims=True)
        o_ref[0, :, sl] = _dot(p.astype(BF16), v) / l
        l_ref[0, :, sl] = jnp.broadcast_to(m + jnp.log(l), (t_len, DH_ATT))


def _attn_sample(p, cache, bias, g, dil):
    nb, t_len, _ = p.shape
    n_buf = cache.shape[1]
    n_pad = bias.shape[2]
    cv = cache.reshape(nb, n_buf, 2 * ATT_W)

    def col(cb):
        return pl.BlockSpec((1, t_len, COL_BLK), lambda n, cb=cb: (n, 0, cb))

    small = pl.BlockSpec((1, t_len, ATT_W), lambda n: (n, 0, 0))
    big = pl.BlockSpec((1, n_buf, 2 * ATT_W), lambda n: (n, 0, 0))
    o, l, nc = pl.pallas_call(
        functools.partial(_attn_s_kernel, n_buf=n_buf, dil=dil),
        grid=(nb,),
        in_specs=[col(CB_AQ + g), col(CB_AK + g), col(CB_AV + g), big,
                  pl.BlockSpec((H_ATT, t_len, n_pad), lambda n: (0, 0, 0))],
        out_specs=[small, small, big],
        out_shape=[jax.ShapeDtypeStruct((nb, t_len, ATT_W), F32)] * 2
                  + [jax.ShapeDtypeStruct((nb, n_buf, 2 * ATT_W), F32)],
        scratch_shapes=[pltpu.VMEM((n_pad, ATT_W), F32), pltpu.VMEM((n_pad, ATT_W), F32)],
        compiler_params=_cparams(1, VMEM_LIMIT),
        name="attn_sample",
    )(p, p, p, cv, bias)
    return o, l, nc


def _merge_kernel(o0, o1, o2, l0, l1, l2, z_ref, gra, grb, gaa, gab, x_ref, gt1_ref, sh2_ref, sc2_ref,
                  wro, wao, wout, bout, lng, lnb, wq, keys, x1_ref, st_ref, il_scr):
    tm = st_ref.shape[1]

    def flat(ref):
        if len(ref.shape) == 4:
            dil = ref.shape[1]
            if dil == 1:
                return ref[0, 0]
            n_c = ATT_W // LANES
            for r in range(dil):
                for c in range(n_c):
                    il_scr[c, pl.ds(r, tm // dil, stride=dil), :] = ref[0, r, :, c * LANES:(c + 1) * LANES]
            return jnp.concatenate([il_scr[c] for c in range(n_c)], axis=-1)
        v = ref[...]
        return v.reshape(tm, v.shape[-1])

    la, lb, lc = flat(l0), flat(l1), flat(l2)
    m = jnp.maximum(jnp.maximum(la, lb), lc)
    ea, eb, ec = jnp.exp(la - m), jnp.exp(lb - m), jnp.exp(lc - m)
    att = (ea * flat(o0) + eb * flat(o1) + ec * flat(o2)) / (ea + eb + ec)
    att_y = _dot(att.astype(BF16), wao[...])
    ret_y = _dot(flat(z_ref).astype(BF16), wro[...])
    g_ret = jnp.concatenate([flat(gra), flat(grb)], axis=-1)
    g_att = jnp.concatenate([flat(gaa), flat(gab)], axis=-1)
    zz = jax.nn.sigmoid(g_ret) * ret_y + jax.nn.sigmoid(g_att) * att_y
    mix = _dot(zz.astype(BF16), wout[...]) + bout[...]
    x = x_ref[...]
    y = ALPHA * x + gt1_ref[...] * mix.reshape(x.shape)
    x1 = _layer_norm(y, lng[...], lnb[...])
    x1_ref[...] = x1
    h2 = (x1 * (1.0 + sc2_ref[...]) + sh2_ref[...]).reshape(tm, D_MODEL)
    q = _dot(h2.astype(BF16), wq[...]).astype(BF16)
    for hs in range(N_HS):
        st_ref[hs * N_KEYS:(hs + 1) * N_KEYS, :] = _dot_nt(keys[hs], q[:, hs * N_KEYS:(hs + 1) * N_KEYS])


def _merge(p, outs, lses, z, x, mod, w, bn, rb):
    nb, r, _ = x.shape
    tm = bn * rb
    n_tok = nb * r

    def rows(width, cb=0):
        return pl.BlockSpec((bn, rb, width), lambda a, b, cb=cb: (a, b, cb))

    def modspec(j):
        return pl.BlockSpec((bn, 1, D_MODEL), lambda a, b, j=j: (a, 0, j))

    def full(arr):
        nd = arr.ndim
        return pl.BlockSpec(arr.shape, lambda a, b, nd=nd: (0,) * nd)

    weights = [w["w_ret_o"], w["w_att_o"], w["w_out"], w["b_out"], w["ln1_g"], w["ln1_b"], w["peer_wq"],
               w["peer_keys"]]
    nrb = r // rb

    def group_spec(arr):
        if arr.ndim == 3:
            return rows(ATT_W)
        dil = arr.shape[1]
        return pl.BlockSpec((1, dil, tm // dil, ATT_W), lambda a, b: (a, 0, b, 0))

    x1, st = pl.pallas_call(
        _merge_kernel,
        grid=(nb // bn, nrb),
        in_specs=[group_spec(a) for a in (*outs, *lses)] + [rows(ATT_W)]
                 + [rows(COL_BLK, CB_GRET), rows(COL_BLK, CB_GRET + 1),
                    rows(COL_BLK, CB_GATT), rows(COL_BLK, CB_GATT + 1),
                    rows(D_MODEL), modspec(2), modspec(3), modspec(4)]
                 + [full(a) for a in weights],
        out_specs=[rows(D_MODEL),
                   pl.BlockSpec((N_HS * N_KEYS, tm), lambda a, b: (0, a * nrb + b))],
        out_shape=[jax.ShapeDtypeStruct((nb, r, D_MODEL), F32),
                   jax.ShapeDtypeStruct((N_HS * N_KEYS, n_tok), F32)],
        scratch_shapes=[pltpu.VMEM((ATT_W // LANES, tm, LANES), F32)],
        compiler_params=_cparams(2, VMEM_LIMIT),
        name="merge",
    )(*outs, *lses, z, p, p, p, p, x, mod, mod, mod, *weights)
    return x1, st


TOPK_TOK = 128
SUB = 8
CAND_ROWS = [TOPK // (i + 1) for i in range(TOPK)]
N_CAND = sum(CAND_ROWS)
N_CAND_PAD = -(-N_CAND // SUB) * SUB
CAND_PAD_IDX = TOPK * TOPK


def _cand_flat_index():
    flat = [i * TOPK + j for i in range(TOPK) for j in range(CAND_ROWS[i])]
    flat += [CAND_PAD_IDX] * (N_CAND_PAD - N_CAND)
    return np.broadcast_to(np.asarray(flat, np.int32)[:, None], (N_CAND_PAD, TOPK_TOK)).copy()


def _argmax_cols(xs, idxs):
    while len(xs) > 1:
        nx, ni = [], []
        for k in range(0, len(xs) - 1, 2):
            nx.append(jnp.maximum(xs[k], xs[k + 1]))
            ni.append(jnp.where(xs[k] >= xs[k + 1], idxs[k], idxs[k + 1]))
        if len(xs) % 2:
            nx.append(xs[-1])
            ni.append(idxs[-1])
        xs, idxs = nx, ni
    v, i = xs[0], idxs[0]
    for shift in (4, 2, 1):
        v2 = pltpu.roll(v, shift, 0)
        i2 = pltpu.roll(i, shift, 0)
        take = (v > v2) | ((v == v2) & (i < i2))
        v = jnp.where(take, v, v2)
        i = jnp.where(take, i, i2)
    return v, i


def _select16(xs, idxs, val_ref, idx_ref):
    xs = list(xs)
    for r in range(TOPK):
        v, i = _argmax_cols(xs, idxs)
        val_ref[pl.ds(r, 1), :] = v[0:1]
        idx_ref[pl.ds(r, 1), :] = i[0:1]
        xs = [jnp.where(ik == i, -jnp.inf, xk) for xk, ik in zip(xs, idxs)]


def _topk_kernel(st_ref, cidx_ref, e_ref, g_ref, sv, si, cand, cv, ci):
    tok = st_ref.shape[1]
    sub_iota = lax.broadcasted_iota(jnp.int32, (SUB, tok), 0)
    key_idx = [sub_iota + SUB * k for k in range(N_KEYS // SUB)]
    cand_idx = [cidx_ref[SUB * k:SUB * (k + 1), :] for k in range(N_CAND_PAD // SUB)]

    def head(h, carry):
        for half in range(2):
            base = pl.multiple_of((2 * h + half) * N_KEYS, N_KEYS)
            xs = [st_ref[pl.ds(base + SUB * k, SUB), :] for k in range(N_KEYS // SUB)]
            _select16(xs, key_idx, sv.at[half], si.at[half])
        a = sv[0]
        b = sv[1]
        cand[N_CAND_PAD - SUB:, :] = jnp.full((SUB, tok), -jnp.inf, F32)
        off = 0
        for i in range(TOPK):
            cand[off:off + CAND_ROWS[i], :] = a[i:i + 1, :] + b[0:CAND_ROWS[i], :]
            off += CAND_ROWS[i]
        _select16([cand[SUB * k:SUB * (k + 1), :] for k in range(N_CAND_PAD // SUB)], cand_idx, cv, ci)
        c_val = cv[...]
        c_idx = ci[...]
        ia = lax.shift_right_logical(c_idx, 4)
        ib = c_idx & (TOPK - 1)
        sa = si[0]
        sb = si[1]
        i0 = jnp.zeros_like(c_idx)
        i1 = jnp.zeros_like(c_idx)
        for i in range(TOPK):
            i0 = jnp.where(ia == i, sa[i:i + 1, :], i0)
            i1 = jnp.where(ib == i, sb[i:i + 1, :], i1)
        ex = jnp.exp(c_val - c_val[0:1, :])
        rows = pl.ds(pl.multiple_of(h * TOPK, TOPK), TOPK)
        e_ref[rows, :] = i0 * N_KEYS + i1
        g_ref[rows, :] = ex / jnp.sum(ex, axis=0, keepdims=True)
        return carry

    lax.fori_loop(0, H_PEER, head, 0)


def _topk(st):
    n_tok = st.shape[1]
    tt = TOPK_TOK
    spec = pl.BlockSpec((N_SEL, tt), lambda i: (0, i))
    return pl.pallas_call(
        _topk_kernel,
        grid=(n_tok // tt,),
        in_specs=[pl.BlockSpec((N_HS * N_KEYS, tt), lambda i: (0, i)),
                  pl.BlockSpec((N_CAND_PAD, tt), lambda i: (0, 0))],
        out_specs=[spec, spec],
        out_shape=[jax.ShapeDtypeStruct((N_SEL, n_tok), jnp.int32),
                   jax.ShapeDtypeStruct((N_SEL, n_tok), F32)],
        scratch_shapes=[pltpu.VMEM((2, TOPK, tt), F32), pltpu.VMEM((2, TOPK, tt), jnp.int32),
                        pltpu.VMEM((N_CAND_PAD, tt), F32),
                        pltpu.VMEM((TOPK, tt), F32), pltpu.VMEM((TOPK, tt), jnp.int32)],
        compiler_params=_cparams(1, VMEM_LIMIT),
        name="topk",
    )(st, jnp.asarray(_cand_flat_index()))


G_LANES = 128
LANES = 128
PV_TOK = 128
HALF = D_MODEL // 2
ROW_SUB = HALF // LANES
HI_MASK = -65536


def _pack_halves(t):
    tb = lax.bitcast_convert_type(t.astype(BF16), jnp.uint16).astype(jnp.uint32)
    word = (tb[:, :HALF] << 16) | tb[:, HALF:]
    return lax.bitcast_convert_type(word, jnp.int32).reshape(-1, LANES)


def _load_table(tab_hbm, tab, sem):
    @pl.when(pl.program_id(0) == 0)
    def _():
        cp = pltpu.make_async_copy(tab_hbm, tab, sem.at[0])
        cp.start()
        cp.wait()


def _gather_rows(e_ref, first, tab, tile):
    for r in range(N_SEL):
        off = pl.multiple_of(e_ref[first + r], ROW_SUB)
        tile[pl.ds(ROW_SUB * r, ROW_SUB), :] = tab[pl.ds(off, ROW_SUB), :]


def _tile_halves(tile, c):
    words = tile[pl.ds(c, N_SEL, stride=ROW_SUB), :]
    hi = lax.bitcast_convert_type(words & HI_MASK, F32)
    lo = lax.bitcast_convert_type(words << 16, F32)
    return hi, lo


TOK_GRP = 16


def _token_loop(e_ref, tab, tiles, compute, finish_group, init):
    _gather_rows(e_ref, 0, tab, tiles[0])

    def body(k, carry):
        grp = k * (TOK_GRP * N_SEL)
        e_grp = e_ref.at[pl.ds(grp, TOK_GRP * N_SEL)]
        outs = []
        for s in range(TOK_GRP):
            if s + 1 < TOK_GRP:
                _gather_rows(e_grp, (s + 1) * N_SEL, tab, tiles[(s + 1) % 2])
            else:
                _gather_rows(e_ref, jnp.minimum(grp + TOK_GRP * N_SEL, (PV_TOK - 1) * N_SEL), tab, tiles[0])
            carry, out = compute(k * TOK_GRP + s, tiles[s % 2], carry)
            outs.append(out)
        finish_group(k, outs)
        return carry

    return lax.fori_loop(0, PV_TOK // TOK_GRP, body, init)


def _pick_token(j):
    tok_row = lax.broadcasted_iota(jnp.int32, (G_LANES, N_SEL), 0)
    return jnp.where(tok_row == j, 1.0, 0.0).astype(BF16)


def _peer_u_kernel(e_ref, gt_ref, x1_ref, sh2_ref, sc2_ref, tab_hbm, w_ref, tab, sem, tile0, tile1, h2_scr):
    _load_table(tab_hbm, tab, sem)
    h2 = x1_ref[...] * (1.0 + sc2_ref[...]) + sh2_ref[...]
    h2_scr[...] = h2.reshape(PV_TOK, D_MODEL)
    ones = jnp.ones((N_SEL, N_SEL), BF16)
    lane = lax.broadcasted_iota(jnp.int32, (N_SEL, G_LANES), 1)

    def compute(j, tile, at):
        x = h2_scr[pl.ds(j, 1), :]
        part = None
        for c in range(ROW_SUB):
            hi, lo = _tile_halves(tile, c)
            t = hi * x[:, c * LANES:(c + 1) * LANES] + lo * x[:, HALF + c * LANES:HALF + (c + 1) * LANES]
            part = t if part is None else part + t
        return jnp.where(lane == j, _dot_exact(part, ones), at), None

    a = _token_loop(e_ref, tab, (tile0, tile1), compute, lambda k, outs: None,
                    jnp.zeros((N_SEL, G_LANES), F32))
    w_ref[...] = gt_ref[...] * (0.5 * a * (1.0 + lax.erf(a * (2.0 ** -0.5))))


def _peer_v_kernel(e_ref, w_ref, x1_ref, gt2_ref, lng, lnb, tab_hbm, o_ref, tab, sem, tile0, tile1, y_scr,
                   w_split):
    _load_table(tab_hbm, tab, sem)
    w_all = w_ref[...]
    w_hi = w_all.astype(BF16)
    r1 = w_all - w_hi.astype(F32)
    w_mid = r1.astype(BF16)
    w_split[0] = w_hi
    w_split[1] = w_mid
    w_split[2] = (r1 - w_mid.astype(F32)).astype(BF16)

    def compute(j, tile, carry):
        pick = _pick_token(j)
        w = _dot(w_split[0], pick) + _dot(w_split[1], pick) + _dot(w_split[2], pick)
        his, los = [], []
        for c in range(ROW_SUB):
            hi, lo = _tile_halves(tile, c)
            his.append(jnp.sum(hi * w, axis=0, keepdims=True))
            los.append(jnp.sum(lo * w, axis=0, keepdims=True))
        return carry, jnp.concatenate(his + los, axis=-1)

    def finish_group(k, rows):
        y_scr[pl.ds(pl.multiple_of(k * TOK_GRP, TOK_GRP), TOK_GRP), :] = jnp.concatenate(rows, axis=0)

    _token_loop(e_ref, tab, (tile0, tile1), compute, finish_group, 0)
    x1 = x1_ref[...]
    y = ALPHA * x1 + gt2_ref[...] * y_scr[...].reshape(x1.shape)
    o_ref[...] = _layer_norm(y, lng[...], lnb[...])


def _peer(e_rows, g_t, x1, mod, tab_u, tab_v, ln_g, ln_b):
    nb, r, _ = x1.shape
    n_tok = nb * r
    steps = n_tok // PV_TOK
    bn, rb = (1, PV_TOK) if r >= PV_TOK else (PV_TOK // r, r)
    nrb = r // rb

    def modspec(j):
        return pl.BlockSpec((bn, 1, D_MODEL), lambda i, j=j: (i // nrb, 0, j))

    e_spec = pl.BlockSpec((PV_TOK * N_SEL,), lambda i: (i,), memory_space=pltpu.SMEM)
    gate_spec = pl.BlockSpec((N_SEL, G_LANES), lambda i: (0, i))
    x_spec = pl.BlockSpec((bn, rb, D_MODEL), lambda i: (i // nrb, i % nrb, 0))
    vec = pl.BlockSpec((1, D_MODEL), lambda i: (0, 0))
    table = pl.BlockSpec(memory_space=pl.ANY)
    tile = pltpu.VMEM((N_SEL * ROW_SUB, LANES), jnp.int32)
    scratch = [pltpu.VMEM(tab_u.shape, jnp.int32), pltpu.SemaphoreType.DMA((1,)), tile, tile,
               pltpu.VMEM((PV_TOK, D_MODEL), F32)]
    w_t = pl.pallas_call(
        _peer_u_kernel,
        grid=(steps,),
        in_specs=[e_spec, gate_spec, x_spec, modspec(3), modspec(4), table],
        out_specs=gate_spec,
        out_shape=jax.ShapeDtypeStruct((N_SEL, n_tok), F32),
        scratch_shapes=scratch,
        compiler_params=_cparams(1, VMEM_LIMIT),
        name="peer_u",
    )(e_rows, g_t, x1, mod, mod, tab_u)
    return pl.pallas_call(
        _peer_v_kernel,
        grid=(steps,),
        in_specs=[e_spec, gate_spec, x_spec, modspec(5), vec, vec, table],
        out_specs=x_spec,
        out_shape=jax.ShapeDtypeStruct((nb, r, D_MODEL), F32),
        scratch_shapes=scratch + [pltpu.VMEM((3, N_SEL, G_LANES), BF16)],
        compiler_params=_cparams(1, VMEM_LIMIT),
        name="peer_v",
    )(e_rows, w_t, x1, mod, ln_g, ln_b, tab_v)


def _branch(x, mod, pos0, s0, caches, biases, w, uv, *, prompt):
    nb, r, _ = x.shape
    if prompt:
        bn, rb_proj, rb_merge, chunk, bn_ret = 1, 1024, 256, RET_CHUNK, 1
    else:
        bn, rb_proj, rb_merge, chunk, bn_ret = 32, r, r, math.gcd(r, RET_CHUNK), 8
    if prompt:
        p, res1, res2 = _inproj_res(x, mod, w["w_in"], rb_proj)
        att_src = ((p.reshape(nb, 1, r, IN_COLS), (CB_AQ, CB_AK, CB_AV)), (res1, (0, 1, 2)), (res2, (0, 1, 2)))
    else:
        p = _inproj(x, mod, w["w_in"], bn, rb_proj)
    cos, sin = _rotary_tables(pos0, r)
    z, s_new = _retention(p, cos, sin, s0, bn_ret, chunk)
    outs, lses, new_bufs = [], [], []
    for g, (win, dil) in enumerate(ATT_GROUPS):
        if prompt:
            o, l = _attn_prompt(att_src[g][0], biases[g], att_src[g][1])
            nw = min(win, r)
            kg = p[:, r - nw:, (CB_AK + g) * COL_BLK:(CB_AK + g + 1) * COL_BLK]
            vg = p[:, r - nw:, (CB_AV + g) * COL_BLK:(CB_AV + g + 1) * COL_BLK]
            nbuf = jnp.stack([kg, vg], axis=2).reshape(nb, nw, 2, H_ATT, DH_ATT)
        else:
            o, l, nc = _attn_sample(p, caches[g], biases[g], g, dil)
            nbuf = nc.reshape(nb, nc.shape[1], 2, H_ATT, DH_ATT)
        outs.append(o)
        lses.append(l)
        new_bufs.append(nbuf)
    x1, st = _merge(p, outs, lses, z, x, mod, w, bn, rb_merge)
    e_t, g_t = _topk(st)
    y = _peer(e_t.T.reshape(-1) * ROW_SUB, g_t, x1, mod, uv[0], uv[1], w["ln2_g"], w["ln2_b"])
    return y, s_new, new_bufs


def kernel(x_prompt, x_sample, state_ret, cache_att_w128, cache_att_w512, cache_att_w2048, c_prompt, c_sample,
           w_ada, b_ada, w_in, w_ret_o, w_att_o, w_out, b_out, ln1_g, ln1_b, peer_wq, peer_keys, peer_u, peer_v,
           ln2_g, ln2_b, rel_bias):
    nbp, seq, _ = x_prompt.shape
    nbs, dec_seq, _ = x_sample.shape
    w = {
        "w_in": w_in[0].astype(BF16),
        "w_ret_o": w_ret_o[0].astype(BF16),
        "w_att_o": w_att_o[0].astype(BF16),
        "w_out": w_out[0].astype(BF16),
        "b_out": b_out[0].reshape(1, -1),
        "ln1_g": ln1_g[0].reshape(1, -1),
        "ln1_b": ln1_b[0].reshape(1, -1),
        "peer_wq": peer_wq[0].astype(BF16),
        "peer_keys": peer_keys[0].reshape(N_HS, N_KEYS, N_KEYS).astype(BF16),
        "ln2_g": ln2_g[0].reshape(1, -1),
        "ln2_b": ln2_b[0].reshape(1, -1),
    }
    uv = (_pack_halves(peer_u[0]), _pack_halves(peer_v[0]))
    mod = _ada(jnp.concatenate([c_prompt, c_sample], axis=0), w_ada[0], b_ada[0])
    mod_p = mod[:nbp].reshape(nbp, 1, -1)
    mod_s = mod[nbp:].reshape(nbs, 1, -1)

    caches = (cache_att_w128[0], cache_att_w512[0], cache_att_w2048[0])
    bias_p, bias_s = [], []
    for g, (win, dil) in enumerate(ATT_GROUPS):
        n_buf = caches[g].shape[1]
        n_pad = n_buf + ATT_SPAN
        idx = np.concatenate([_prompt_bucket_idx(dil), _sample_bucket_idx(n_buf, n_pad, dec_seq)], axis=1)
        tab = _bias_lookup(rel_bias[:, g * H_ATT:(g + 1) * H_ATT].T, jnp.asarray(idx))
        n_p = ATT_SPAN * 2 * ATT_SPAN
        bias_p.append(tab[:, :n_p].reshape(H_ATT, ATT_SPAN, 2 * ATT_SPAN))
        bias_s.append(tab[:, n_p:].reshape(H_ATT, dec_seq, n_pad))

    s0_p = jnp.zeros((nbp, H_RET, DK_RET, DK_RET), F32)
    ys, rs, bs = _branch(x_sample, mod_s, PAST_LEN, state_ret[0], caches, bias_s, w, uv, prompt=False)
    yp, rp, bp = _branch(x_prompt, mod_p, 0, s0_p, None, bias_p, w, uv, prompt=True)
    return (yp, ys, rp[None], bp[0][None], bp[1][None], bp[2][None],
            rs[None], bs[0][None], bs[1][None], bs[2][None])
```
